```python
import math
import jax, jax.numpy as jnp
from jax import lax
import numpy as np

D_MODEL = 1024
BATCH = 4
SEQ = 8192
DEPTH = 1

DA_HEADS = 8
DA_HEAD_DIM = 64
DA_V_DIM = 2 * DA_HEAD_DIM
DA_QK_W = DA_HEADS * 2 * DA_HEAD_DIM
DA_V_W = DA_HEADS * DA_V_DIM
Q_BLOCK = 128
DIL_PAIRS = ((128, 1), (512, 4), (2048, 16))
DIL_HEADS_PER_GROUP = 4
DIL_HEADS = DIL_HEADS_PER_GROUP * len(DIL_PAIRS)
DIL_HEAD_DIM = 64
DIL_W = DIL_HEADS * DIL_HEAD_DIM
DIL_OUT_W = DIL_HEADS_PER_GROUP * DIL_HEAD_DIM
N_BRANCH = 2
IN_COLS = 2 * DA_QK_W + DA_V_W + 3 * DIL_W + N_BRANCH * D_MODEL
ROPE_THETA = 500000.0
ROT_DIM = 16
D_FF = 2816
EPS = 1e-6
NEG = -1e30

kernel_name = "hybrid_diffattn_dilated_macaron_block"


def rmsnorm(x, g):
    xf = x.astype(jnp.float32)
    y = xf * lax.rsqrt(jnp.mean(xf * xf, axis=-1, keepdims=True) + EPS)
    return (y * g.astype(jnp.float32)).astype(x.dtype)


def rope_tables(positions, dtype):
    inv = ROPE_THETA ** (-(jnp.arange(0, ROT_DIM, 2, dtype=jnp.float32) / ROT_DIM))
    ang = positions.astype(jnp.float32)[..., None] * inv
    return jnp.cos(ang)[:, :, None, :].astype(dtype), jnp.sin(ang)[:, :, None, :].astype(dtype)


def partial_rope(t, cos, sin):
    half = ROT_DIM // 2
    t1, t2, tp = t[..., :half], t[..., half:ROT_DIM], t[..., ROT_DIM:]
    return jnp.concatenate([t1 * cos - t2 * sin, t2 * cos + t1 * sin, tp], axis=-1)


def swiglu(h, w_gu, w_down):
    g, u = jnp.split(h @ w_gu, 2, axis=-1)
    return (jax.nn.silu(g) * u) @ w_down


def diff_attention(q, k, v, lam):
    B, S, H, _, dh = q.shape
    nblk = S // Q_BLOCK
    scale = 1.0 / math.sqrt(dh)
    qb = q.reshape(B, nblk, Q_BLOCK, H, 2, dh).transpose(1, 0, 2, 3, 4, 5)

    def block(qblk):
        s = jnp.einsum('bqhcd,bkhcd->bhcqk', qblk, k, preferred_element_type=jnp.float32) * scale
        p = jax.nn.softmax(s, axis=-1)
        a = p[:, :, 0] - lam * p[:, :, 1]
        return jnp.einsum('bhqk,bkhe->bqhe', a.astype(v.dtype), v,
                          preferred_element_type=jnp.float32).astype(v.dtype)

    o = lax.map(block, qb)
    return o.transpose(1, 0, 2, 3, 4).reshape(B, S, H, 2 * dh)


def dilated_window_attention(q, k, v, dil, half):
    B, S, Hg, dh = q.shape
    L = S // dil
    N = B * dil
    blk = half
    nb = -(-L // blk)
    Lp = nb * blk
    scale = 1.0 / math.sqrt(dh)

    def fold(t):
        return t.reshape(B, L, dil, Hg, dh).transpose(0, 2, 1, 3, 4).reshape(N, L, Hg, dh)

    qf, kf, vf = fold(q), fold(k), fold(v)
    qb = jnp.pad(qf, ((0, 0), (0, Lp - L), (0, 0), (0, 0))).reshape(N, nb, blk, Hg, dh)

    def ctx(t):
        tp = jnp.pad(t, ((0, 0), (blk, Lp - L + blk), (0, 0), (0, 0))).reshape(N, nb + 2, blk, Hg, dh)
        return jnp.concatenate([tp[:, :-2], tp[:, 1:-1], tp[:, 2:]], axis=2)

    kc, vc = ctx(kf), ctx(vf)
    s = jnp.einsum('nbqhd,nbkhd->nbhqk', qb, kc, preferred_element_type=jnp.float32) * scale
    qpos = jnp.arange(nb)[:, None] * blk + jnp.arange(blk)[None, :]
    kpos = (jnp.arange(nb)[:, None] - 1) * blk + jnp.arange(3 * blk)[None, :]
    valid = ((jnp.abs(kpos[:, None, :] - qpos[:, :, None]) <= half)
             & (kpos >= 0)[:, None, :] & (kpos < L)[:, None, :])
    s = jnp.where(valid[None, :, None], s, NEG)
    m = jnp.max(s, axis=-1, keepdims=True)
    e = jnp.exp(s - m)
    l = jnp.sum(e, axis=-1)
    o = jnp.einsum('nbhqk,nbkhd->nbqhd', e.astype(v.dtype), vc, preferred_element_type=jnp.float32)
    o = o / l.transpose(0, 1, 3, 2)[..., None]
    lse = (m[..., 0] + jnp.log(l)).transpose(0, 1, 3, 2)

    def unfold(t):
        t = t.reshape((N, Lp) + t.shape[3:])[:, :L]
        t = t.reshape((B, dil, L) + t.shape[2:])
        return jnp.swapaxes(t, 1, 2).reshape((B, S) + t.shape[3:])

    return unfold(o).astype(q.dtype), unfold(lse)


def setup_inputs(seed: int = 0) -> dict:
    key = jax.random.key(seed)
    ks = jax.random.split(key, 24)
    f32 = jnp.float32

    def w(k, shape, fan_in):
        return jax.random.normal(k, shape, f32) * fan_in ** -0.5

    def gain(k, shape):
        return 1.0 + 0.05 * jax.random.normal(k, shape, f32)

    return {
        "x": jax.random.normal(ks[0], (BATCH, SEQ, D_MODEL), f32),
        "positions": jnp.broadcast_to(jnp.arange(SEQ, dtype=jnp.int32), (BATCH, SEQ)),
        "w_in": w(ks[1], (DEPTH, D_MODEL, IN_COLS), D_MODEL),
        "lambda_q1": 0.1 * jax.random.normal(ks[2], (DEPTH, DA_HEAD_DIM), f32),
        "lambda_k1": 0.1 * jax.random.normal(ks[3], (DEPTH, DA_HEAD_DIM), f32),
        "lambda_q2": 0.1 * jax.random.normal(ks[4], (DEPTH, DA_HEAD_DIM), f32),
        "lambda_k2": 0.1 * jax.random.normal(ks[5], (DEPTH, DA_HEAD_DIM), f32),
        "g_subln": gain(ks[6], (DEPTH, DA_V_DIM)),
        "w_proj_a": w(ks[7], (DEPTH, DA_V_W, D_MODEL), DA_V_W),
        "w_proj_b": w(ks[8], (DEPTH, DIL_OUT_W, D_MODEL), DIL_OUT_W),
        "w_out": w(ks[9], (DEPTH, D_MODEL, D_MODEL), D_MODEL),
        "w_gu1": w(ks[10], (DEPTH, D_MODEL, 2 * D_FF), D_MODEL),
        "w_down1": w(ks[11], (DEPTH, D_FF, D_MODEL), D_FF),
        "w_gu2": w(ks[12], (DEPTH, D_MODEL, 2 * D_FF), D_MODEL),
        "w_down2": w(ks[13], (DEPTH, D_FF, D_MODEL), D_FF),
        "g_pre_ffn1": gain(ks[14], (DEPTH, D_MODEL)),
        "g_post_ffn1": gain(ks[15], (DEPTH, D_MODEL)),
        "g_pre_mix": gain(ks[16], (DEPTH, D_MODEL)),
        "g_post_mix": gain(ks[17], (DEPTH, D_MODEL)),
        "g_pre_ffn2": gain(ks[18], (DEPTH, D_MODEL)),
        "g_post_ffn2": gain(ks[19], (DEPTH, D_MODEL)),
    }


def reference(x, positions, w_in, lambda_q1, lambda_k1, lambda_q2, lambda_k2, g_subln,
              w_proj_a, w_proj_b, w_out, w_gu1, w_down1, w_gu2, w_down2,
              g_pre_ffn1, g_post_ffn1, g_pre_mix, g_post_mix, g_pre_ffn2, g_post_ffn2):
    B, S, D = x.shape
    cos, sin = rope_tables(positions, x.dtype)
    cuts = [int(c) for c in np.cumsum([DA_QK_W, DA_QK_W, DA_V_W, DIL_W, DIL_W, DIL_W, D_MODEL])]

    for l in range(DEPTH):
        lambda_init = 0.8 - 0.6 * math.exp(-0.3 * l)

        h = rmsnorm(x, g_pre_ffn1[l])
        x = x + 0.5 * rmsnorm(swiglu(h, w_gu1[l], w_down1[l]), g_post_ffn1[l])

        h = rmsnorm(x, g_pre_mix[l])
        z = h @ w_in[l]
        qa, ka, va, qd, kd, vd, ga, gb = jnp.split(z, cuts, axis=-1)

        qa = partial_rope(qa.reshape(B, S, 2 * DA_HEADS, DA_HEAD_DIM), cos, sin)
        ka = partial_rope(ka.reshape(B, S, 2 * DA_HEADS, DA_HEAD_DIM), cos, sin)
        qa = qa.reshape(B, S, DA_HEADS, 2, DA_HEAD_DIM)
        ka = ka.reshape(B, S, DA_HEADS, 2, DA_HEAD_DIM)
        va = va.reshape(B, S, DA_HEADS, DA_V_DIM)
        lam = (jnp.exp(jnp.dot(lambda_q1[l], lambda_k1[l]).astype(jnp.float32))
               - jnp.exp(jnp.dot(lambda_q2[l], lambda_k2[l]).astype(jnp.float32)) + lambda_init)
        oa = diff_attention(qa, ka, va, lam)
        oa = (rmsnorm(oa, g_subln[l]) * (1.0 - lambda_init)).reshape(B, S, DA_V_W)

        qd = partial_rope(qd.reshape(B, S, DIL_HEADS, DIL_HEAD_DIM), cos, sin)
        kd = partial_rope(kd.reshape(B, S, DIL_HEADS, DIL_HEAD_DIM), cos, sin)
        vd = vd.reshape(B, S, DIL_HEADS, DIL_HEAD_DIM)
        outs, lses = [], []
        for gi, (win, dil) in enumerate(DIL_PAIRS):
            hs = slice(gi * DIL_HEADS_PER_GROUP, (gi + 1) * DIL_HEADS_PER_GROUP)
            o_g, lse_g = dilated_window_attention(qd[:, :, hs], kd[:, :, hs], vd[:, :, hs],
                                                  dil, win // (2 * dil))
            outs.append(o_g)
            lses.append(lse_g)
        wts = jax.nn.softmax(jnp.stack(lses, axis=0), axis=0)
        od = jnp.sum(wts[..., None].astype(x.dtype) * jnp.stack(outs, axis=0), axis=0)
        od = od.reshape(B, S, DIL_OUT_W)

        merged = jax.nn.sigmoid(ga) * (oa @ w_proj_a[l]) + jax.nn.sigmoid(gb) * (od @ w_proj_b[l])
        x = x + rmsnorm(merged @ w_out[l], g_post_mix[l])

        h = rmsnorm(x, g_pre_ffn2[l])
        x = x + 0.5 * rmsnorm(swiglu(h, w_gu2[l], w_down2[l]), g_post_ffn2[l])

    return x
```

```python
import functools
import math

import jax
import jax.numpy as jnp
from jax import lax
from jax.experimental import pallas as pl
from jax.experimental.pallas import tpu as pltpu

D_MODEL = 1024
DA_HEADS = 8
DA_HEAD_DIM = 64
DA_V_DIM = 2 * DA_HEAD_DIM
DA_QK_W = DA_HEADS * 2 * DA_HEAD_DIM
DA_V_W = DA_HEADS * DA_V_DIM
DIL_PAIRS = ((128, 1), (512, 4), (2048, 16))
DIL_HEADS_PER_GROUP = 4
DIL_HEAD_DIM = 64
DIL_GROUP_W = DIL_HEADS_PER_GROUP * DIL_HEAD_DIM
DIL_W = DIL_GROUP_W * len(DIL_PAIRS)
ROPE_THETA = 500000.0
ROT_DIM = 16
D_FF = 2816
EPS = 1e-6
NEG = -1e30

LANES = 128
VMEM_LIMIT = 56 * 1024 * 1024

FFN_TM = 256
TOK_TM = 512
ATT_TQ = 512
DIL_TL = 512

_BF16 = jnp.bfloat16
_F32 = jnp.float32


def _params(n_axes):
    return pltpu.CompilerParams(dimension_semantics=("arbitrary",) * n_axes,
                                vmem_limit_bytes=VMEM_LIMIT)


def _resident(shape):
    zeros = (0,) * len(shape)
    return pl.BlockSpec(shape, lambda *_: zeros, pipeline_mode=pl.Buffered(1))


def _rms(x, g):
    ms = jnp.mean(x * x, axis=-1, keepdims=True)
    return x * lax.rsqrt(ms + EPS) * g


def _ffn_body(x_ref, gpre_ref, wgu_ref, wd_ref, gpost_ref, o_ref):
    x = x_ref[...]
    h = _rms(x, gpre_ref[...]).astype(_BF16)
    gu = jnp.dot(h, wgu_ref[...], preferred_element_type=_F32)
    g = gu[:, :D_FF]
    u = gu[:, D_FF:]
    a = (g * jax.nn.sigmoid(g) * u).astype(_BF16)
    y = jnp.dot(a, wd_ref[...], preferred_element_type=_F32)
    o_ref[...] = x + 0.5 * _rms(y, gpost_ref[...])


def _ffn(x, g_pre, w_gu, w_down, g_post):
    n, d = x.shape
    tm = min(FFN_TM, n)
    row = pl.BlockSpec((tm, d), lambda i: (i, 0))
    return pl.pallas_call(
        _ffn_body,
        grid=(n // tm,),
        in_specs=[row, _resident((1, d)), _resident(w_gu.shape), _resident(w_down.shape),
                  _resident((1, d))],
        out_specs=row,
        out_shape=jax.ShapeDtypeStruct((n, d), _F32),
        compiler_params=_params(1),
        name="ffn",
    )(x, g_pre.reshape(1, d), w_gu.astype(_BF16), w_down.astype(_BF16), g_post.reshape(1, d))


def _inproj_body(x_ref, g_ref, cos_ref, sa_ref, sb_ref, w_ref, wvt_ref,
                 q_ref, k_ref, vt_ref, qd_ref, kd_ref, vd_ref, ga_ref, gb_ref, *, q_scale, qd_scale):
    h = _rms(x_ref[...], g_ref[...]).astype(_BF16)
    cos, sa, sb = cos_ref[...], sa_ref[...], sb_ref[...]

    def proj(start, width):
        return jnp.dot(h, w_ref[:, start:start + width], preferred_element_type=_F32)

    def rope_store(z, out_ref, scale):
        for j in range(z.shape[1] // LANES):
            zj = z[:, j * LANES:(j + 1) * LANES]
            r = (zj * cos + pltpu.roll(zj, ROT_DIM // 2, 1) * sa
                 + pltpu.roll(zj, LANES - ROT_DIM // 2, 1) * sb)
            out_ref[:, j * LANES:(j + 1) * LANES] = (r * scale).astype(out_ref.dtype)

    c = 0
    rope_store(proj(c, DA_QK_W), q_ref, q_scale); c += DA_QK_W
    rope_store(proj(c, DA_QK_W), k_ref, 1.0); c += DA_QK_W
    vt_ref[...] = lax.dot_general(wvt_ref[...], h, (((1,), (1,)), ((), ())),
                                  preferred_element_type=_F32).astype(vt_ref.dtype)
    c += DA_V_W
    rope_store(proj(c, DIL_W), qd_ref, qd_scale); c += DIL_W
    rope_store(proj(c, DIL_W), kd_ref, 1.0); c += DIL_W
    vd_ref[...] = proj(c, DIL_W).astype(vd_ref.dtype); c += DIL_W
    ga_ref[...] = proj(c, D_MODEL); c += D_MODEL
    gb_ref[...] = proj(c, D_MODEL)


def _inproj(x, g_pre, w_in, rope, batch, seq):
    n, d = x.shape
    tm = min(TOK_TM, seq)
    ns = seq // tm
    cos, sa, sb = rope
    w = w_in.astype(_BF16)
    v0 = 2 * DA_QK_W
    wvt = w[:, v0:v0 + DA_V_W].T
    row = lambda width: pl.BlockSpec((tm, width), lambda i: (i, 0))
    body = functools.partial(
        _inproj_body,
        q_scale=math.log2(math.e) / math.sqrt(DA_HEAD_DIM),
        qd_scale=1.0 / math.sqrt(DIL_HEAD_DIM))
    return pl.pallas_call(
        body,
        grid=(n // tm,),
        in_specs=[row(d), _resident((1, d)), row(LANES), row(LANES), row(LANES),
                  _resident(w.shape), _resident(wvt.shape)],
        out_specs=[row(DA_QK_W), row(DA_QK_W),
                   pl.BlockSpec((None, None, DA_V_W, tm), lambda i: (i // ns, i % ns, 0, 0)),
                   row(DIL_W), row(DIL_W), row(DIL_W), row(d), row(d)],
        out_shape=[jax.ShapeDtypeStruct((n, DA_QK_W), _BF16),
                   jax.ShapeDtypeStruct((n, DA_QK_W), _BF16),
                   jax.ShapeDtypeStruct((batch, ns, DA_V_W, tm), _BF16),
                   jax.ShapeDtypeStruct((n, DIL_W), _BF16),
                   jax.ShapeDtypeStruct((n, DIL_W), _BF16),
                   jax.ShapeDtypeStruct((n, DIL_W), _BF16),
                   jax.ShapeDtypeStruct((n, d), _F32),
                   jax.ShapeDtypeStruct((n, d), _F32)],
        compiler_params=_params(1),
        name="inproj",
    )(x, g_pre.reshape(1, d), cos, sa, sb, w, wvt)


def _diffattn_body(lam_ref, gsub_ref, q_ref, k_ref, vt_ref, o_ref, m_ref, l_ref, acc_ref,
                   *, lambda_init):
    nk, _, tk = vt_ref.shape
    q = q_ref[...]
    lane = lax.broadcasted_iota(jnp.int32, q.shape, 1)
    zero = jnp.zeros_like(q)
    qc = (jnp.where(lane < DA_HEAD_DIM, q, zero), jnp.where(lane >= DA_HEAD_DIM, q, zero))
    m_ref[...] = jnp.full(m_ref.shape, NEG, _F32)
    l_ref[...] = jnp.zeros(l_ref.shape, _F32)
    acc_ref[...] = jnp.zeros(acc_ref.shape, _F32)

    def step(j, carry):
        kt = k_ref[pl.ds(pl.multiple_of(j * tk, tk), tk), :]
        vt = vt_ref[j]
        for c in range(2):
            s = lax.dot_general(kt, qc[c], (((1,), (1,)), ((), ())),
                                preferred_element_type=_F32)
            m_old = m_ref[c]
            m_new = jnp.maximum(m_old, jnp.max(s, axis=0, keepdims=True))
            alpha = jnp.exp2(m_old - m_new)
            p = jnp.exp2(s - m_new)
            l_ref[c] = alpha * l_ref[c] + jnp.sum(p, axis=0, keepdims=True)
            acc_ref[c] = alpha * acc_ref[c] + jnp.dot(vt, p.astype(_BF16),
                                                      preferred_element_type=_F32)
            m_ref[c] = m_new
        return carry

    lax.fori_loop(0, nk, step, 0)

    lq1, lk1, lq2, lk2 = (lam_ref[i:i + 1, :] for i in range(4))
    lam = (jnp.exp(jnp.sum(lq1 * lk1, axis=1, keepdims=True))
           - jnp.exp(jnp.sum(lq2 * lk2, axis=1, keepdims=True)) + lambda_init)
    o = acc_ref[0] / l_ref[0] - lam * (acc_ref[1] / l_ref[1])
    ms = jnp.mean(o * o, axis=0, keepdims=True)
    y = o * lax.rsqrt(ms + EPS) * gsub_ref[...] * (1.0 - lambda_init)
    o_ref[...] = y.T.astype(o_ref.dtype)


def _diffattn(q, k, vt, lam_vecs, g_subln, lambda_init):
    b, s, _ = q.shape
    tq = min(ATT_TQ, s)
    nk, tk = vt.shape[1], vt.shape[3]
    body = functools.partial(_diffattn_body, lambda_init=lambda_init)
    return pl.pallas_call(
        body,
        grid=(b, DA_HEADS, s // tq),
        in_specs=[_resident(lam_vecs.shape), _resident((DA_V_DIM, 1)),
                  pl.BlockSpec((None, tq, DA_V_DIM), lambda bi, h, qi: (bi, qi, h)),
                  pl.BlockSpec((None, s, DA_V_DIM), lambda bi, h, qi: (bi, 0, h)),
                  pl.BlockSpec((None, nk, DA_V_DIM, tk), lambda bi, h, qi: (bi, 0, h, 0))],
        out_specs=pl.BlockSpec((None, tq, DA_V_DIM), lambda bi, h, qi: (bi, qi, h)),
        out_shape=jax.ShapeDtypeStruct((b, s, DA_V_W), _BF16),
        scratch_shapes=[pltpu.VMEM((2, 1, tq), _F32), pltpu.VMEM((2, 1, tq), _F32),
                        pltpu.VMEM((2, DA_V_DIM, tq), _F32)],
        compiler_params=_params(3),
        name="diffattn",
    )(lam_vecs, g_subln.reshape(DA_V_DIM, 1), q, k, vt)


def _dilattn_body(q_ref, kp_ref, km_ref, kn_ref, vp_ref, vm_ref, vn_ref, o_ref, lse_ref,
                  *, half, seq_len):
    tl = q_ref.shape[0]
    qb, win = 2 * half, 4 * half
    nh = DIL_HEADS_PER_GROUP
    base = pl.program_id(2) * tl
    q = q_ref[...]
    kfull = jnp.concatenate([kp_ref[...], km_ref[...], kn_ref[...]], axis=0)
    vfull = jnp.concatenate([vp_ref[...], vm_ref[...], vn_ref[...]], axis=0)
    head_of_lane = lax.broadcasted_iota(jnp.int32, (qb, DIL_GROUP_W), 1) // DIL_HEAD_DIM
    r = lax.broadcasted_iota(jnp.int32, (nh * qb, win), 0) % qb
    j = lax.broadcasted_iota(jnp.int32, (nh * qb, win), 1)
    band = (j >= r) & (j <= r + 2 * half)
    for i in range(tl // qb):
        qi = q[i * qb:(i + 1) * qb]
        kw = kfull[i * qb:i * qb + win]
        vw = vfull[i * qb:i * qb + win]
        kpos = base + (i * qb - half) + j
        valid = band & (kpos >= 0) & (kpos < seq_len)
        zero = jnp.zeros_like(qi)
        qs = jnp.concatenate([jnp.where(head_of_lane == h, qi, zero) for h in range(nh)], axis=0)
        s = lax.dot_general(qs, kw, (((1,), (1,)), ((), ())), preferred_element_type=_F32)
        s = jnp.where(valid, s, NEG)
        m = jnp.max(s, axis=1, keepdims=True)
        e = jnp.exp(s - m)
        l = jnp.sum(e, axis=1, keepdims=True)
        res = jnp.dot(e.astype(_BF16), vw, preferred_element_type=_F32) / l
        lse = m + jnp.log(l)
        o = jnp.zeros((qb, DIL_GROUP_W), _F32)
        lse_full = jnp.zeros((qb, DIL_GROUP_W), _F32)
        for h in range(nh):
            sel = head_of_lane == h
            o = jnp.where(sel, res[h * qb:(h + 1) * qb], o)
            lse_full = jnp.where(sel, lse[h * qb:(h + 1) * qb], lse_full)
        o_ref[i * qb:(i + 1) * qb, :] = o
        lse_ref[i * qb:(i + 1) * qb, :] = lse_full


def _dilattn(qd, kd, vd, group, dil, half):
    b, s, _ = qd.shape
    seq_len = s // dil
    ngroups = DIL_W // DIL_GROUP_W
    tl = min(DIL_TL, seq_len)
    assert seq_len % tl == 0 and tl % (2 * half) == 0 and half % 16 == 0
    per_tile = tl // half
    last = seq_len // half - 1
    view = lambda t: t.reshape(b, seq_len, dil * DIL_W)
    col = lambda r: r * ngroups + group
    main = pl.BlockSpec((None, tl, DIL_GROUP_W), lambda bi, r, li: (bi, li, col(r)))
    prev = pl.BlockSpec((None, half, DIL_GROUP_W),
                        lambda bi, r, li: (bi, jnp.maximum(li * per_tile - 1, 0), col(r)))
    nxt = pl.BlockSpec((None, half, DIL_GROUP_W),
                       lambda bi, r, li: (bi, jnp.minimum((li + 1) * per_tile, last), col(r)))
    out = pl.BlockSpec((None, tl, DIL_GROUP_W), lambda bi, r, li: (bi, li, r))
    body = functools.partial(_dilattn_body, half=half, seq_len=seq_len)
    o, lse = pl.pallas_call(
        body,
        grid=(b, dil, seq_len // tl),
        in_specs=[main, prev, main, nxt, prev, main, nxt],
        out_specs=[out, out],
        out_shape=[jax.ShapeDtypeStruct((b, seq_len, dil * DIL_GROUP_W), _F32)] * 2,
        compiler_params=_params(3),
        name=f"dilattn_d{dil}",
    )(view(qd), view(kd), view(kd), view(kd), view(vd), view(vd), view(vd))
    return o.reshape(b * s, DIL_GROUP_W), lse.reshape(b * s, DIL_GROUP_W)


def _merge_body(x_ref, oa_ref, o0_ref, o1_ref, o2_ref, l0_ref, l1_ref, l2_ref, ga_ref, gb_ref,
                wa_ref, wb_ref, wo_ref, g_ref, out_ref):
    lses = (l0_ref[...], l1_ref[...], l2_ref[...])
    outs = (o0_ref[...], o1_ref[...], o2_ref[...])
    m = jnp.maximum(jnp.maximum(lses[0], lses[1]), lses[2])
    es = [jnp.exp(l - m) for l in lses]
    den = es[0] + es[1] + es[2]
    od = (es[0] / den) * outs[0] + (es[1] / den) * outs[1] + (es[2] / den) * outs[2]
    pa = jnp.dot(oa_ref[...], wa_ref[...], preferred_element_type=_F32)
    pb = jnp.dot(od.astype(_BF16), wb_ref[...], preferred_element_type=_F32)
    merged = jax.nn.sigmoid(ga_ref[...]) * pa + jax.nn.sigmoid(gb_ref[...]) * pb
    mo = jnp.dot(merged.astype(_BF16), wo_ref[...], preferred_element_type=_F32)
    out_ref[...] = x_ref[...] + _rms(mo, g_ref[...])


def _merge(x, oa, dil_outs, dil_lses, ga, gb, w_proj_a, w_proj_b, w_out, g_post):
    n, d = x.shape
    tm = min(TOK_TM, n)
    row = lambda width: pl.BlockSpec((tm, width), lambda i: (i, 0))
    return pl.pallas_call(
        _merge_body,
        grid=(n // tm,),
        in_specs=[row(d), row(DA_V_W)] + [row(DIL_GROUP_W)] * 6 + [row(d), row(d),
                  _resident(w_proj_a.shape), _resident(w_proj_b.shape), _resident(w_out.shape),
                  _resident((1, d))],
        out_specs=row(d),
        out_shape=jax.ShapeDtypeStruct((n, d), _F32),
        compiler_params=_params(1),
        name="merge",
    )(x, oa, *dil_outs, *dil_lses, ga, gb, w_proj_a.astype(_BF16), w_proj_b.astype(_BF16),
      w_out.astype(_BF16), g_post.reshape(1, d))


def _rope_patterns(positions):
    half = ROT_DIM // 2
    inv = ROPE_THETA ** (-(jnp.arange(0, ROT_DIM, 2, dtype=_F32) / ROT_DIM))
    ang = positions.astype(_F32).reshape(-1, 1) * inv
    cos, sin = jnp.cos(ang), jnp.sin(ang)
    n = ang.shape[0]
    rest = DA_HEAD_DIM - ROT_DIM
    reps = LANES // DA_HEAD_DIM
    cos_p = jnp.concatenate([cos, cos, jnp.ones((n, rest), _F32)], axis=1)
    sa_p = jnp.concatenate([jnp.zeros((n, half), _F32), sin, jnp.zeros((n, rest), _F32)], axis=1)
    sb_p = jnp.concatenate([-sin, jnp.zeros((n, half + rest), _F32)], axis=1)
    return tuple(jnp.tile(t, (1, reps)) for t in (cos_p, sa_p, sb_p))


def kernel(x, positions, w_in, lambda_q1, lambda_k1, lambda_q2, lambda_k2, g_subln, w_proj_a, w_proj_b, w_out, w_gu1, w_down1, w_gu2, w_down2, g_pre_ffn1, g_post_ffn1, g_pre_mix, g_post_mix, g_pre_ffn2, g_post_ffn2):
    b, s, d = x.shape
    n = b * s
    depth = w_in.shape[0]
    rope = _rope_patterns(positions)
    xf = x.reshape(n, d)
    for l in range(depth):
        lambda_init = 0.8 - 0.6 * math.exp(-0.3 * l)
        xf = _ffn(xf, g_pre_ffn1[l], w_gu1[l], w_down1[l], g_post_ffn1[l])

        q, k, vt, qd, kd, vd, ga, gb = _inproj(xf, g_pre_mix[l], w_in[l], rope, b, s)
        lam_vecs = jnp.stack([lambda_q1[l], lambda_k1[l], lambda_q2[l], lambda_k2[l]], axis=0)
        oa = _diffattn(q.reshape(b, s, DA_QK_W), k.reshape(b, s, DA_QK_W), vt, lam_vecs,
                       g_subln[l], lambda_init)

        qd3, kd3, vd3 = (t.reshape(b, s, DIL_W) for t in (qd, kd, vd))
        dil_outs, dil_lses = [], []
        for gi, (win, dil) in enumerate(DIL_PAIRS):
            o_g, lse_g = _dilattn(qd3, kd3, vd3, gi, dil, win // (2 * dil))
            dil_outs.append(o_g)
            dil_lses.append(lse_g)

        xf = _merge(xf, oa.reshape(n, DA_V_W), dil_outs, dil_lses, ga, gb,
                    w_proj_a[l], w_proj_b[l], w_out[l], g_post_mix[l])
        xf = _ffn(xf, g_pre_ffn2[l], w_gu2[l], w_down2[l], g_post_ffn2[l])
    return xf.reshape(b, s, d)
```

```python
import functools
import math

import jax
import jax.numpy as jnp
from jax import lax
from jax.experimental import pallas as pl
from jax.experimental.pallas import tpu as pltpu

D_MODEL = 1024
DA_HEADS = 8
DA_HEAD_DIM = 64
DA_V_DIM = 2 * DA_HEAD_DIM
DA_QK_W = DA_HEADS * 2 * DA_HEAD_DIM
DA_V_W = DA_HEADS * DA_V_DIM
DIL_PAIRS = ((128, 1), (512, 4), (2048, 16))
DIL_HEADS_PER_GROUP = 4
DIL_HEAD_DIM = 64
DIL_GROUP_W = DIL_HEADS_PER_GROUP * DIL_HEAD_DIM
DIL_W = DIL_GROUP_W * len(DIL_PAIRS)
ROPE_THETA = 500000.0
ROT_DIM = 16
D_FF = 2816
EPS = 1e-6
NEG = -1e30

LANES = 128
VMEM_LIMIT = 56 * 1024 * 1024

FFN_TM = 256
TOK_TM = 512
ATT_TQ = 512
DIL_T = 2048
DIL_UNROLL = 4

_BF16 = jnp.bfloat16
_F32 = jnp.float32


def _params(n_axes):
    return pltpu.CompilerParams(dimension_semantics=("arbitrary",) * n_axes,
                                vmem_limit_bytes=VMEM_LIMIT)


def _resident(shape):
    zeros = (0,) * len(shape)
    return pl.BlockSpec(shape, lambda *_: zeros, pipeline_mode=pl.Buffered(1))


def _rms(x, g):
    ms = jnp.mean(x * x, axis=-1, keepdims=True)
    return x * lax.rsqrt(ms + EPS) * g


def _ffn_body(x_ref, gpre_ref, wgu_ref, wd_ref, gpost_ref, o_ref):
    x = x_ref[...]
    h = _rms(x, gpre_ref[...]).astype(_BF16)
    gu = jnp.dot(h, wgu_ref[...], preferred_element_type=_F32)
    g = gu[:, :D_FF]
    u = gu[:, D_FF:]
    a = (g * jax.nn.sigmoid(g) * u).astype(_BF16)
    y = jnp.dot(a, wd_ref[...], preferred_element_type=_F32)
    o_ref[...] = x + 0.5 * _rms(y, gpost_ref[...])


def _ffn(x, g_pre, w_gu, w_down, g_post):
    n, d = x.shape
    tm = min(FFN_TM, n)
    row = pl.BlockSpec((tm, d), lambda i: (i, 0))
    return pl.pallas_call(
        _ffn_body,
        grid=(n // tm,),
        in_specs=[row, _resident((1, d)), _resident(w_gu.shape), _resident(w_down.shape),
                  _resident((1, d))],
        out_specs=row,
        out_shape=jax.ShapeDtypeStruct((n, d), _F32),
        compiler_params=_params(1),
        name="ffn",
    )(x, g_pre.reshape(1, d), w_gu.astype(_BF16), w_down.astype(_BF16), g_post.reshape(1, d))


def _inproj_body(x_ref, g_ref, cos_ref, sa_ref, sb_ref, w_ref, wvt_ref,
                 q_ref, k_ref, vt_ref, qd_ref, kd_ref, vd_ref, ga_ref, gb_ref, *, q_scale, qd_scale):
    h = _rms(x_ref[...], g_ref[...]).astype(_BF16)
    cos, sa, sb = cos_ref[...], sa_ref[...], sb_ref[...]

    def proj(start, width):
        return jnp.dot(h, w_ref[:, start:start + width], preferred_element_type=_F32)

    def rope(zj):
        return (zj * cos + pltpu.roll(zj, ROT_DIM // 2, 1) * sa
                + pltpu.roll(zj, LANES - ROT_DIM // 2, 1) * sb)

    def rope_store(z, out_ref, scale):
        for j in range(z.shape[1] // LANES):
            r = rope(z[:, j * LANES:(j + 1) * LANES]) * scale
            out_ref[:, j * LANES:(j + 1) * LANES] = r.astype(out_ref.dtype)

    def slab_store(z, out_ref, fn):
        for j in range(z.shape[1] // LANES):
            out_ref[j] = fn(z[:, j * LANES:(j + 1) * LANES])

    c = 0
    rope_store(proj(c, DA_QK_W), q_ref, q_scale); c += DA_QK_W
    rope_store(proj(c, DA_QK_W), k_ref, 1.0); c += DA_QK_W
    vt_ref[...] = lax.dot_general(wvt_ref[...], h, (((1,), (1,)), ((), ())),
                                  preferred_element_type=_F32).astype(vt_ref.dtype)
    c += DA_V_W
    slab_store(proj(c, DIL_W), qd_ref, lambda zj: rope(zj) * qd_scale); c += DIL_W
    slab_store(proj(c, DIL_W), kd_ref, rope); c += DIL_W
    slab_store(proj(c, DIL_W), vd_ref, lambda zj: zj); c += DIL_W
    ga_ref[...] = proj(c, D_MODEL); c += D_MODEL
    gb_ref[...] = proj(c, D_MODEL)


def _inproj(x, g_pre, w_in, rope, batch, seq):
    n, d = x.shape
    tm = min(TOK_TM, seq)
    ns = seq // tm
    cos, sa, sb = rope
    w = w_in.astype(_BF16)
    v0 = 2 * DA_QK_W
    wvt = w[:, v0:v0 + DA_V_W].T
    row = lambda width: pl.BlockSpec((tm, width), lambda i: (i, 0))
    nslab = DIL_W // LANES
    slabs = pl.BlockSpec((nslab, tm, LANES), lambda i: (0, i, 0))
    body = functools.partial(
        _inproj_body,
        q_scale=math.log2(math.e) / math.sqrt(DA_HEAD_DIM),
        qd_scale=1.0 / math.sqrt(DIL_HEAD_DIM))
    return pl.pallas_call(
        body,
        grid=(n // tm,),
        in_specs=[row(d), _resident((1, d)), row(LANES), row(LANES), row(LANES),
                  _resident(w.shape), _resident(wvt.shape)],
        out_specs=[row(DA_QK_W), row(DA_QK_W),
                   pl.BlockSpec((None, None, DA_V_W, tm), lambda i: (i // ns, i % ns, 0, 0)),
                   slabs, slabs, slabs, row(d), row(d)],
        out_shape=[jax.ShapeDtypeStruct((n, DA_QK_W), _BF16),
                   jax.ShapeDtypeStruct((n, DA_QK_W), _BF16),
                   jax.ShapeDtypeStruct((batch, ns, DA_V_W, tm), _BF16),
                   jax.ShapeDtypeStruct((nslab, n, LANES), _F32),
                   jax.ShapeDtypeStruct((nslab, n, LANES), _F32),
                   jax.ShapeDtypeStruct((nslab, n, LANES), _F32),
                   jax.ShapeDtypeStruct((n, d), _F32),
                   jax.ShapeDtypeStruct((n, d), _F32)],
        compiler_params=_params(1),
        name="inproj",
    )(x, g_pre.reshape(1, d), cos, sa, sb, w, wvt)


def _diffattn_body(lam_ref, gsub_ref, q_ref, k_ref, vt_ref, o_ref,
                   s_ref, mx_ref, m_ref, l_ref, acc_ref, *, lambda_init):
    nk, _, tk = vt_ref.shape
    q = q_ref[...]
    lane = lax.broadcasted_iota(jnp.int32, q.shape, 1)
    zero = jnp.zeros_like(q)
    qc = (jnp.where(lane < DA_HEAD_DIM, q, zero), jnp.where(lane >= DA_HEAD_DIM, q, zero))
    m_ref[...] = jnp.full(m_ref.shape, NEG, _F32)
    l_ref[...] = jnp.zeros(l_ref.shape, _F32)
    acc_ref[...] = jnp.zeros(acc_ref.shape, _F32)

    def scores(j, slot):
        kt = k_ref[pl.ds(pl.multiple_of(j * tk, tk), tk), :]
        for c in range(2):
            s = lax.dot_general(kt, qc[c], (((1,), (1,)), ((), ())),
                                preferred_element_type=_F32)
            s_ref[slot, c] = s
            mx_ref[slot, c] = jnp.max(s, axis=0, keepdims=True)

    def consume(j, slot):
        vt = vt_ref[j]
        for c in range(2):
            m_old = m_ref[c]
            m_new = jnp.maximum(m_old, mx_ref[slot, c])
            alpha = jnp.exp2(m_old - m_new)
            p = jnp.exp2(s_ref[slot, c] - m_new)
            l_ref[c] = alpha * l_ref[c] + jnp.sum(p, axis=0, keepdims=True)
            acc_ref[c] = alpha * acc_ref[c] + jnp.dot(vt, p.astype(_BF16),
                                                      preferred_element_type=_F32)
            m_ref[c] = m_new

    scores(0, 0)

    def pair(jj, carry):
        j0 = 2 * jj
        scores(j0 + 1, 1)
        consume(j0, 0)
        scores(j0 + 2, 0)
        consume(j0 + 1, 1)
        return carry

    lax.fori_loop(0, nk // 2 - 1, pair, 0)
    scores(nk - 1, 1)
    consume(nk - 2, 0)
    consume(nk - 1, 1)

    lq1, lk1, lq2, lk2 = (lam_ref[i:i + 1, :] for i in range(4))
    lam = (jnp.exp(jnp.sum(lq1 * lk1, axis=1, keepdims=True))
           - jnp.exp(jnp.sum(lq2 * lk2, axis=1, keepdims=True)) + lambda_init)
    o = acc_ref[0] / l_ref[0] - lam * (acc_ref[1] / l_ref[1])
    ms = jnp.mean(o * o, axis=0, keepdims=True)
    y = o * lax.rsqrt(ms + EPS) * gsub_ref[...] * (1.0 - lambda_init)
    o_ref[...] = y.T.astype(o_ref.dtype)


def _diffattn(q, k, vt, lam_vecs, g_subln, lambda_init):
    b, s, _ = q.shape
    tq = min(ATT_TQ, s)
    nk, tk = vt.shape[1], vt.shape[3]
    assert nk >= 2 and nk % 2 == 0 and s % tq == 0
    body = functools.partial(_diffattn_body, lambda_init=lambda_init)
    return pl.pallas_call(
        body,
        grid=(b, DA_HEADS, s // tq),
        in_specs=[_resident(lam_vecs.shape), _resident((DA_V_DIM, 1)),
                  pl.BlockSpec((None, tq, DA_V_DIM), lambda bi, h, qi: (bi, qi, h)),
                  pl.BlockSpec((None, s, DA_V_DIM), lambda bi, h, qi: (bi, 0, h)),
                  pl.BlockSpec((None, nk, DA_V_DIM, tk), lambda bi, h, qi: (bi, 0, h, 0))],
        out_specs=pl.BlockSpec((None, tq, DA_V_DIM), lambda bi, h, qi: (bi, qi, h)),
        out_shape=jax.ShapeDtypeStruct((b, s, DA_V_W), _BF16),
        scratch_shapes=[pltpu.VMEM((2, 2, tk, tq), _F32),
                        pltpu.VMEM((2, 2, 1, tq), _F32),
                        pltpu.VMEM((2, 1, tq), _F32), pltpu.VMEM((2, 1, tq), _F32),
                        pltpu.VMEM((2, DA_V_DIM, tq), _F32)],
        compiler_params=_params(3),
        name="diffattn",
    )(lam_vecs, g_subln.reshape(DA_V_DIM, 1), q, k, vt)


def _dilattn_body(q_ref, kp_ref, km_ref, kn_ref, vp_ref, vm_ref, vn_ref, o_ref, lse_ref,
                  *, dil, half, seq_len, tiles_per_seq):
    t = q_ref.shape[1]
    per_res = t // dil
    qb, win = 2 * half, 4 * half
    n_sub = per_res // qb
    nh = DIL_HEADS_PER_GROUP
    nslab = DIL_GROUP_W // LANES
    base = (pl.program_id(0) % tiles_per_seq) * per_res
    head_of_lane = lax.broadcasted_iota(jnp.int32, (qb, DIL_GROUP_W), 1) // DIL_HEAD_DIM
    r_iota = lax.broadcasted_iota(jnp.int32, (nh * qb, win), 0) % qb
    j = lax.broadcasted_iota(jnp.int32, (nh * qb, win), 1)
    band_bias = jnp.where((j >= r_iota) & (j <= r_iota + 2 * half), 0.0, NEG).astype(_F32)
    j_row = lax.broadcasted_iota(jnp.int32, (1, win), 1)

    def rows(start, count):
        return pl.ds(start, count, stride=dil) if dil > 1 else pl.ds(start, count)

    def gather(ref, start, count):
        return jnp.concatenate([ref[sl, rows(start, count), :] for sl in range(nslab)],
                               axis=1).astype(_BF16)

    def window(prev_ref, main_ref, next_ref, res, i, row0):
        lo_main = gather(main_ref, jnp.maximum(row0 - half * dil, res), half)
        lo = jnp.where(i == 0, gather(prev_ref, res, half), lo_main)
        hi_main = gather(main_ref, jnp.minimum(row0 + qb * dil, (per_res - half) * dil + res),
                         half)
        hi = jnp.where(i == n_sub - 1, gather(next_ref, res, half), hi_main)
        return jnp.concatenate([lo, gather(main_ref, row0, qb), hi], axis=0)

    def block(idx, carry):
        res, i = idx // n_sub, idx % n_sub
        row0 = i * (qb * dil) + res
        qi = gather(q_ref, row0, qb)
        kw = window(kp_ref, km_ref, kn_ref, res, i, row0)
        vw = window(vp_ref, vm_ref, vn_ref, res, i, row0)
        kpos = base + (i * qb - half) + j_row
        in_seq = (kpos >= 0) & (kpos < seq_len)
        zero = jnp.zeros_like(qi)
        qs = jnp.concatenate([jnp.where(head_of_lane == h, qi, zero) for h in range(nh)], axis=0)
        s = lax.dot_general(qs, kw, (((1,), (1,)), ((), ())), preferred_element_type=_F32)
        s = jnp.where(in_seq, s + band_bias, NEG)
        m = jnp.max(s, axis=1, keepdims=True)
        e = jnp.exp(s - m)
        l = jnp.sum(e, axis=1, keepdims=True)
        pv = jnp.dot(e.astype(_BF16), vw, preferred_element_type=_F32) / l
        lse = m + jnp.log(l)
        o = jnp.zeros((qb, DIL_GROUP_W), _F32)
        lse_full = jnp.zeros((qb, DIL_GROUP_W), _F32)
        for h in range(nh):
            sel = head_of_lane == h
            o = jnp.where(sel, pv[h * qb:(h + 1) * qb], o)
            lse_full = jnp.where(sel, lse[h * qb:(h + 1) * qb], lse_full)
        for sl in range(nslab):
            o_ref[sl, rows(row0, qb), :] = o[:, sl * LANES:(sl + 1) * LANES]
            lse_ref[sl, rows(row0, qb), :] = lse_full[:, sl * LANES:(sl + 1) * LANES]
        return carry

    lax.fori_loop(0, dil * n_sub, block, 0, unroll=DIL_UNROLL)


def _dilattn(qd, kd, vd, group, dil, half, seq):
    nslab_all, n, _ = qd.shape
    nslab = DIL_GROUP_W // LANES
    t = min(DIL_T, seq)
    halo = half * dil
    assert seq % t == 0 and t % (2 * half * dil) == 0 and t % halo == 0 and half % 8 == 0
    tiles_per_seq = seq // t
    per_tile = t // halo
    last = n // halo - 1
    main = pl.BlockSpec((nslab, t, LANES), lambda i: (group, i, 0))
    prev = pl.BlockSpec((nslab, halo, LANES),
                        lambda i: (group, jnp.maximum(i * per_tile - 1, 0), 0))
    nxt = pl.BlockSpec((nslab, halo, LANES),
                       lambda i: (group, jnp.minimum((i + 1) * per_tile, last), 0))
    out = pl.BlockSpec((nslab, t, LANES), lambda i: (0, i, 0))
    body = functools.partial(_dilattn_body, dil=dil, half=half, seq_len=seq // dil,
                             tiles_per_seq=tiles_per_seq)
    return pl.pallas_call(
        body,
        grid=(n // t,),
        in_specs=[main, prev, main, nxt, prev, main, nxt],
        out_specs=[out, out],
        out_shape=[jax.ShapeDtypeStruct((nslab, n, LANES), _F32)] * 2,
        compiler_params=_params(1),
        name=f"dilattn_d{dil}",
    )(qd, kd, kd, kd, vd, vd, vd)


def _merge_body(x_ref, oa_ref, o0_ref, o1_ref, o2_ref, l0_ref, l1_ref, l2_ref, ga_ref, gb_ref,
                wa_ref, wb_ref, wo_ref, g_ref, out_ref):
    od_slabs = []
    for sl in range(DIL_GROUP_W // LANES):
        lses = (l0_ref[sl], l1_ref[sl], l2_ref[sl])
        outs = (o0_ref[sl], o1_ref[sl], o2_ref[sl])
        m = jnp.maximum(jnp.maximum(lses[0], lses[1]), lses[2])
        es = [jnp.exp(l - m) for l in lses]
        den = es[0] + es[1] + es[2]
        od_slabs.append((es[0] / den) * outs[0] + (es[1] / den) * outs[1]
                        + (es[2] / den) * outs[2])
    od = jnp.concatenate(od_slabs, axis=1)
    pa = jnp.dot(oa_ref[...], wa_ref[...], preferred_element_type=_F32)
    pb = jnp.dot(od.astype(_BF16), wb_ref[...], preferred_element_type=_F32)
    merged = jax.nn.sigmoid(ga_ref[...]) * pa + jax.nn.sigmoid(gb_ref[...]) * pb
    mo = jnp.dot(merged.astype(_BF16), wo_ref[...], preferred_element_type=_F32)
    out_ref[...] = x_ref[...] + _rms(mo, g_ref[...])


def _merge(x, oa, dil_outs, dil_lses, ga, gb, w_proj_a, w_proj_b, w_out, g_post):
    n, d = x.shape
    tm = min(TOK_TM, n)
    row = lambda width: pl.BlockSpec((tm, width), lambda i: (i, 0))
    slabs = pl.BlockSpec((DIL_GROUP_W // LANES, tm, LANES), lambda i: (0, i, 0))
    return pl.pallas_call(
        _merge_body,
        grid=(n // tm,),
        in_specs=[row(d), row(DA_V_W)] + [slabs] * 6 + [row(d), row(d),
                  _resident(w_proj_a.shape), _resident(w_proj_b.shape), _resident(w_out.shape),
                  _resident((1, d))],
        out_specs=row(d),
        out_shape=jax.ShapeDtypeStruct((n, d), _F32),
        compiler_params=_params(1),
        name="merge",
    )(x, oa, *dil_outs, *dil_lses, ga, gb, w_proj_a.astype(_BF16), w_proj_b.astype(_BF16),
      w_out.astype(_BF16), g_post.reshape(1, d))


def _rope_patterns(positions):
    half = ROT_DIM // 2
    inv = ROPE_THETA ** (-(jnp.arange(0, ROT_DIM, 2, dtype=_F32) / ROT_DIM))
    ang = positions.astype(_F32).reshape(-1, 1) * inv
    cos, sin = jnp.cos(ang), jnp.sin(ang)
    n = ang.shape[0]
    rest = DA_HEAD_DIM - ROT_DIM
    reps = LANES // DA_HEAD_DIM
    cos_p = jnp.concatenate([cos, cos, jnp.ones((n, rest), _F32)], axis=1)
    sa_p = jnp.concatenate([jnp.zeros((n, half), _F32), sin, jnp.zeros((n, rest), _F32)], axis=1)
    sb_p = jnp.concatenate([-sin, jnp.zeros((n, half + rest), _F32)], axis=1)
    return tuple(jnp.tile(t, (1, reps)) for t in (cos_p, sa_p, sb_p))


def kernel(x, positions, w_in, lambda_q1, lambda_k1, lambda_q2, lambda_k2, g_subln, w_proj_a, w_proj_b, w_out, w_gu1, w_down1, w_gu2, w_down2, g_pre_ffn1, g_post_ffn1, g_pre_mix, g_post_mix, g_pre_ffn2, g_post_ffn2):
    b, s, d = x.shape
    n = b * s
    depth = w_in.shape[0]
    rope = _rope_patterns(positions)
    xf = x.reshape(n, d)
    for l in range(depth):
        lambda_init = 0.8 - 0.6 * math.exp(-0.3 * l)
        xf = _ffn(xf, g_pre_ffn1[l], w_gu1[l], w_down1[l], g_post_ffn1[l])

        q, k, vt, qd, kd, vd, ga, gb = _inproj(xf, g_pre_mix[l], w_in[l], rope, b, s)
        lam_vecs = jnp.stack([lambda_q1[l], lambda_k1[l], lambda_q2[l], lambda_k2[l]], axis=0)
        oa = _diffattn(q.reshape(b, s, DA_QK_W), k.reshape(b, s, DA_QK_W), vt, lam_vecs,
                       g_subln[l], lambda_init)

        dil_outs, dil_lses = [], []
        for gi, (win, dil) in enumerate(DIL_PAIRS):
            o_g, lse_g = _dilattn(qd, kd, vd, gi, dil, win // (2 * dil), s)
            dil_outs.append(o_g)
            dil_lses.append(lse_g)

        xf = _merge(xf, oa.reshape(n, DA_V_W), dil_outs, dil_lses, ga, gb,
                    w_proj_a[l], w_proj_b[l], w_out[l], g_post_mix[l])
        xf = _ffn(xf, g_pre_ffn2[l], w_gu2[l], w_down2[l], g_post_ffn2[l])
    return xf.reshape(b, s, d)
```

```python
import functools
import math

import jax
import jax.numpy as jnp
from jax import lax
from jax.experimental import pallas as pl
from jax.experimental.pallas import tpu as pltpu

D_MODEL = 1024
DA_HEADS = 8
DA_HEAD_DIM = 64
DA_V_DIM = 2 * DA_HEAD_DIM
DA_QK_W = DA_HEADS * 2 * DA_HEAD_DIM
DA_V_W = DA_HEADS * DA_V_DIM
DIL_PAIRS = ((128, 1), (512, 4), (2048, 16))
DIL_HEADS_PER_GROUP = 4
DIL_HEAD_DIM = 64
DIL_GROUP_W = DIL_HEADS_PER_GROUP * DIL_HEAD_DIM
DIL_W = DIL_GROUP_W * len(DIL_PAIRS)
ROPE_THETA = 500000.0
ROT_DIM = 16
D_FF = 2816
EPS = 1e-6
NEG = -1e30

LANES = 128
BF16_ROWS = 16
VMEM_LIMIT = 56 * 1024 * 1024

FFN_TM = 512
FFN_CHUNKS = 2
TOK_TM = 512
ATT_TQ = 512
ATT_TK = 1024
DIL_T = 2048
DIL_UNROLL = 4

_BF16 = jnp.bfloat16
_F32 = jnp.float32


def _params(n_axes):
    return pltpu.CompilerParams(dimension_semantics=("arbitrary",) * n_axes,
                                vmem_limit_bytes=VMEM_LIMIT)


def _resident(shape):
    zeros = (0,) * len(shape)
    return pl.BlockSpec(shape, lambda *_: zeros, pipeline_mode=pl.Buffered(1))


def _rms(x, g):
    ms = jnp.mean(x * x, axis=-1, keepdims=True)
    return x * lax.rsqrt(ms + EPS) * g


def _ffn_body(x_ref, gpre_ref, wgu_ref, wd_ref, gpost_ref, o_ref):
    rows = x_ref.shape[0] // FFN_CHUNKS
    for i in range(FFN_CHUNKS):
        x = x_ref[i * rows:(i + 1) * rows, :]
        h = _rms(x, gpre_ref[...]).astype(_BF16)
        gu = jnp.dot(h, wgu_ref[...], preferred_element_type=_F32)
        g = gu[:, :D_FF]
        u = gu[:, D_FF:]
        a = (g * jax.nn.sigmoid(g) * u).astype(_BF16)
        y = jnp.dot(a, wd_ref[...], preferred_element_type=_F32)
        o_ref[i * rows:(i + 1) * rows, :] = x + 0.5 * _rms(y, gpost_ref[...])


def _ffn(x, g_pre, w_gu, w_down, g_post):
    n, d = x.shape
    tm = min(FFN_TM, n)
    row = pl.BlockSpec((tm, d), lambda i: (i, 0))
    return pl.pallas_call(
        _ffn_body,
        grid=(n // tm,),
        in_specs=[row, _resident((1, d)), _resident(w_gu.shape), _resident(w_down.shape),
                  _resident((1, d))],
        out_specs=row,
        out_shape=jax.ShapeDtypeStruct((n, d), _F32),
        compiler_params=_params(1),
        name="ffn",
    )(x, g_pre.reshape(1, d), w_gu.astype(_BF16), w_down.astype(_BF16), g_post.reshape(1, d))


def _inproj_body(x_ref, g_ref, cos_ref, sa_ref, sb_ref, w_ref, wvt_ref,
                 q_ref, k_ref, vt_ref, qd_ref, kd_ref, vd_ref, ga_ref, gb_ref, *, q_scale, qd_scale):
    h = _rms(x_ref[...], g_ref[...]).astype(_BF16)
    cos, sa, sb = cos_ref[...], sa_ref[...], sb_ref[...]

    def proj(start, width):
        return jnp.dot(h, w_ref[:, start:start + width], preferred_element_type=_F32)

    def rope(zj):
        return (zj * cos + pltpu.roll(zj, ROT_DIM // 2, 1) * sa
                + pltpu.roll(zj, LANES - ROT_DIM // 2, 1) * sb)

    def rope_store(z, out_ref, scale):
        for j in range(z.shape[1] // LANES):
            r = rope(z[:, j * LANES:(j + 1) * LANES]) * scale
            out_ref[:, j * LANES:(j + 1) * LANES] = r.astype(out_ref.dtype)

    def slab_store(z, out_ref, fn):
        for j in range(z.shape[1] // LANES):
            out_ref[j] = fn(z[:, j * LANES:(j + 1) * LANES])

    c = 0
    rope_store(proj(c, DA_QK_W), q_ref, q_scale); c += DA_QK_W
    rope_store(proj(c, DA_QK_W), k_ref, 1.0); c += DA_QK_W
    vt_ref[...] = lax.dot_general(wvt_ref[...], h, (((1,), (1,)), ((), ())),
                                  preferred_element_type=_F32).astype(vt_ref.dtype)
    c += DA_V_W
    slab_store(proj(c, DIL_W), qd_ref, lambda zj: rope(zj) * qd_scale); c += DIL_W
    slab_store(proj(c, DIL_W), kd_ref, rope); c += DIL_W
    slab_store(proj(c, DIL_W), vd_ref, lambda zj: zj); c += DIL_W
    ga_ref[...] = proj(c, D_MODEL); c += D_MODEL
    gb_ref[...] = proj(c, D_MODEL)


def _inproj(x, g_pre, w_in, rope, batch, seq):
    n, d = x.shape
    tm = min(TOK_TM, seq)
    ns = seq // tm
    tk = min(ATT_TK, seq)
    per_key_tile = tk // tm
    cos, sa, sb = rope
    w = w_in.astype(_BF16)
    v0 = 2 * DA_QK_W
    wvt = w[:, v0:v0 + DA_V_W].T
    row = lambda width: pl.BlockSpec((tm, width), lambda i: (i, 0))
    nslab = DIL_W // LANES
    slabs = pl.BlockSpec((nslab, tm, LANES), lambda i: (0, i, 0))
    body = functools.partial(
        _inproj_body,
        q_scale=math.log2(math.e) / math.sqrt(DA_HEAD_DIM),
        qd_scale=1.0 / math.sqrt(DIL_HEAD_DIM))
    return pl.pallas_call(
        body,
        grid=(n // tm,),
        in_specs=[row(d), _resident((1, d)), row(LANES), row(LANES), row(LANES),
                  _resident(w.shape), _resident(wvt.shape)],
        out_specs=[row(DA_QK_W), row(DA_QK_W),
                   pl.BlockSpec((None, None, DA_V_W, tm),
                                lambda i: (i // ns, (i % ns) // per_key_tile, 0, i % per_key_tile)),
                   slabs, slabs, slabs, row(d), row(d)],
        out_shape=[jax.ShapeDtypeStruct((n, DA_QK_W), _BF16),
                   jax.ShapeDtypeStruct((n, DA_QK_W), _BF16),
                   jax.ShapeDtypeStruct((batch, seq // tk, DA_V_W, tk), _BF16),
                   jax.ShapeDtypeStruct((nslab, n, LANES), _F32),
                   jax.ShapeDtypeStruct((nslab, n, LANES), _F32),
                   jax.ShapeDtypeStruct((nslab, n, LANES), _F32),
                   jax.ShapeDtypeStruct((n, d), _F32),
                   jax.ShapeDtypeStruct((n, d), _F32)],
        compiler_params=_params(1),
        name="inproj",
    )(x, g_pre.reshape(1, d), cos, sa, sb, w, wvt)


def _diffattn_body(lam_ref, gsub_ref, q_ref, k_ref, vt_ref, o_ref,
                   s_ref, mx_ref, m_ref, acc_ref, *, tq, lambda_init):
    nk, _, tk = vt_ref.shape
    total = (q_ref.shape[0] // tq) * nk
    lane = lax.broadcasted_iota(jnp.int32, (tq, DA_V_DIM), 1)
    ones_rows = (lax.broadcasted_iota(jnp.int32, (BF16_ROWS, tk), 0) == 0).astype(_BF16)
    m_ref[...] = jnp.zeros(m_ref.shape, _F32)
    acc_ref[...] = jnp.zeros(acc_ref.shape, _F32)

    def scores(t, slot):
        t = jnp.minimum(t, total - 1)
        q = q_ref[pl.ds(pl.multiple_of((t // nk) * tq, tq), tq), :]
        kt = k_ref[pl.ds(pl.multiple_of((t % nk) * tk, tk), tk), :]
        zero = jnp.zeros_like(q)
        for c in range(2):
            qc = jnp.where((lane >= c * DA_HEAD_DIM) & (lane < (c + 1) * DA_HEAD_DIM), q, zero)
            s = lax.dot_general(kt, qc, (((1,), (1,)), ((), ())),
                                preferred_element_type=_F32)
            s_ref[slot, c] = s
            mx_ref[slot, c] = jnp.max(s, axis=0, keepdims=True)

    def consume(t, slot, may_start):
        j = t % nk
        vt = jnp.concatenate([vt_ref[j], ones_rows], axis=0)
        for c in range(2):
            m_old = m_ref[c]
            if may_start:
                m_old = jnp.where(j == 0, NEG, m_old)
            m_new = jnp.maximum(m_old, mx_ref[slot, c])
            alpha = jnp.exp2(m_old - m_new)
            p = jnp.exp2(s_ref[slot, c] - m_new).astype(_BF16)
            acc_ref[c] = alpha * acc_ref[c] + jnp.dot(vt, p, preferred_element_type=_F32)
            m_ref[c] = m_new

    def finalize(qi):
        lq1, lk1, lq2, lk2 = (lam_ref[i:i + 1, :] for i in range(4))
        lam = (jnp.exp(jnp.sum(lq1 * lk1, axis=1, keepdims=True))
               - jnp.exp(jnp.sum(lq2 * lk2, axis=1, keepdims=True)) + lambda_init)
        num = [acc_ref[c, :DA_V_DIM, :] for c in range(2)]
        den = [acc_ref[c, DA_V_DIM:DA_V_DIM + 1, :] for c in range(2)]
        o = num[0] / den[0] - lam * (num[1] / den[1])
        ms = jnp.mean(o * o, axis=0, keepdims=True)
        y = o * lax.rsqrt(ms + EPS) * gsub_ref[...] * (1.0 - lambda_init)
        o_ref[pl.ds(pl.multiple_of(qi * tq, tq), tq), :] = y.T.astype(o_ref.dtype)

    scores(0, 0)

    def pair(u, carry):
        t0 = 2 * u
        scores(t0 + 1, 1)
        consume(t0, 0, True)
        scores(t0 + 2, 0)
        consume(t0 + 1, 1, False)

        @pl.when((t0 + 2) % nk == 0)
        def _():
            finalize(t0 // nk)

        return carry

    lax.fori_loop(0, total // 2, pair, 0)


def _diffattn(q, k, vt, lam_vecs, g_subln, lambda_init):
    b, s, _ = q.shape
    tq = min(ATT_TQ, s)
    nk, tk = vt.shape[1], vt.shape[3]
    assert nk % 2 == 0 and s % tq == 0
    body = functools.partial(_diffattn_body, tq=tq, lambda_init=lambda_init)
    head = pl.BlockSpec((None, s, DA_V_DIM), lambda bi, h: (bi, 0, h))
    return pl.pallas_call(
        body,
        grid=(b, DA_HEADS),
        in_specs=[_resident(lam_vecs.shape), _resident((DA_V_DIM, 1)), head, head,
                  pl.BlockSpec((None, nk, DA_V_DIM, tk), lambda bi, h: (bi, 0, h, 0))],
        out_specs=head,
        out_shape=jax.ShapeDtypeStruct((b, s, DA_V_W), _BF16),
        scratch_shapes=[pltpu.VMEM((2, 2, tk, tq), _F32),
                        pltpu.VMEM((2, 2, 1, tq), _F32),
                        pltpu.VMEM((2, 1, tq), _F32),
                        pltpu.VMEM((2, DA_V_DIM + BF16_ROWS, tq), _F32)],
        compiler_params=_params(2),
        name="diffattn",
    )(lam_vecs, g_subln.reshape(DA_V_DIM, 1), q, k, vt)


def _dilattn_body(q_ref, kp_ref, km_ref, kn_ref, vp_ref, vm_ref, vn_ref, o_ref, lse_ref,
                  *, dil, half, seq_len, tiles_per_seq):
    t = q_ref.shape[1]
    per_res = t // dil
    qb, win = 2 * half, 4 * half
    n_sub = per_res // qb
    nh = DIL_HEADS_PER_GROUP
    nslab = DIL_GROUP_W // LANES
    base = (pl.program_id(0) % tiles_per_seq) * per_res
    head_of_lane = lax.broadcasted_iota(jnp.int32, (qb, DIL_GROUP_W), 1) // DIL_HEAD_DIM
    r_iota = lax.broadcasted_iota(jnp.int32, (nh * qb, win), 0) % qb
    j = lax.broadcasted_iota(jnp.int32, (nh * qb, win), 1)
    band_bias = jnp.where((j >= r_iota) & (j <= r_iota + 2 * half), 0.0, NEG).astype(_F32)
    j_row = lax.broadcasted_iota(jnp.int32, (1, win), 1)

    def rows(start, count):
        return pl.ds(start, count, stride=dil) if dil > 1 else pl.ds(start, count)

    def gather(ref, start, count):
        return jnp.concatenate([ref[sl, rows(start, count), :] for sl in range(nslab)],
                               axis=1).astype(_BF16)

    def window(prev_ref, main_ref, next_ref, res, i, row0):
        lo_main = gather(main_ref, jnp.maximum(row0 - half * dil, res), half)
        lo = jnp.where(i == 0, gather(prev_ref, res, half), lo_main)
        hi_main = gather(main_ref, jnp.minimum(row0 + qb * dil, (per_res - half) * dil + res),
                         half)
        hi = jnp.where(i == n_sub - 1, gather(next_ref, res, half), hi_main)
        return jnp.concatenate([lo, gather(main_ref, row0, qb), hi], axis=0)

    def block(idx, carry):
        res, i = idx // n_sub, idx % n_sub
        row0 = i * (qb * dil) + res
        qi = gather(q_ref, row0, qb)
        kw = window(kp_ref, km_ref, kn_ref, res, i, row0)
        vw = window(vp_ref, vm_ref, vn_ref, res, i, row0)
        kpos = base + (i * qb - half) + j_row
        in_seq = (kpos >= 0) & (kpos < seq_len)
        zero = jnp.zeros_like(qi)
        qs = jnp.concatenate([jnp.where(head_of_lane == h, qi, zero) for h in range(nh)], axis=0)
        s = lax.dot_general(qs, kw, (((1,), (1,)), ((), ())), preferred_element_type=_F32)
        s = jnp.where(in_seq, s + band_bias, NEG)
        m = jnp.max(s, axis=1, keepdims=True)
        e = jnp.exp(s - m)
        l = jnp.sum(e, axis=1, keepdims=True)
        pv = jnp.dot(e.astype(_BF16), vw, preferred_element_type=_F32) / l
        lse = m + jnp.log(l)
        o = jnp.zeros((qb, DIL_GROUP_W), _F32)
        lse_full = jnp.zeros((qb, DIL_GROUP_W), _F32)
        for h in range(nh):
            sel = head_of_lane == h
            o = jnp.where(sel, pv[h * qb:(h + 1) * qb], o)
            lse_full = jnp.where(sel, lse[h * qb:(h + 1) * qb], lse_full)
        for sl in range(nslab):
            o_ref[sl, rows(row0, qb), :] = o[:, sl * LANES:(sl + 1) * LANES]
            lse_ref[sl, rows(row0, qb), :] = lse_full[:, sl * LANES:(sl + 1) * LANES]
        return carry

    lax.fori_loop(0, dil * n_sub, block, 0, unroll=DIL_UNROLL)


def _dilattn(qd, kd, vd, group, dil, half, seq):
    nslab_all, n, _ = qd.shape
    nslab = DIL_GROUP_W // LANES
    t = min(DIL_T, seq)
    halo = half * dil
    assert seq % t == 0 and t % (2 * half * dil) == 0 and t % halo == 0 and half % 8 == 0
    tiles_per_seq = seq // t
    per_tile = t // halo
    last = n // halo - 1
    main = pl.BlockSpec((nslab, t, LANES), lambda i: (group, i, 0))
    prev = pl.BlockSpec((nslab, halo, LANES),
                        lambda i: (group, jnp.maximum(i * per_tile - 1, 0), 0))
    nxt = pl.BlockSpec((nslab, halo, LANES),
                       lambda i: (group, jnp.minimum((i + 1) * per_tile, last), 0))
    out = pl.BlockSpec((nslab, t, LANES), lambda i: (0, i, 0))
    body = functools.partial(_dilattn_body, dil=dil, half=half, seq_len=seq // dil,
                             tiles_per_seq=tiles_per_seq)
    return pl.pallas_call(
        body,
        grid=(n // t,),
        in_specs=[main, prev, main, nxt, prev, main, nxt],
        out_specs=[out, out],
        out_shape=[jax.ShapeDtypeStruct((nslab, n, LANES), _F32)] * 2,
        compiler_params=_params(1),
        name=f"dilattn_d{dil}",
    )(qd, kd, kd, kd, vd, vd, vd)


def _merge_body(x_ref, oa_ref, o0_ref, o1_ref, o2_ref, l0_ref, l1_ref, l2_ref, ga_ref, gb_ref,
                wa_ref, wb_ref, wo_ref, g_ref, out_ref):
    od_slabs = []
    for sl in range(DIL_GROUP_W // LANES):
        lses = (l0_ref[sl], l1_ref[sl], l2_ref[sl])
        outs = (o0_ref[sl], o1_ref[sl], o2_ref[sl])
        m = jnp.maximum(jnp.maximum(lses[0], lses[1]), lses[2])
        es = [jnp.exp(l - m) for l in lses]
        den = es[0] + es[1] + es[2]
        od_slabs.append((es[0] / den) * outs[0] + (es[1] / den) * outs[1]
                        + (es[2] / den) * outs[2])
    od = jnp.concatenate(od_slabs, axis=1)
    pa = jnp.dot(oa_ref[...], wa_ref[...], preferred_element_type=_F32)
    pb = jnp.dot(od.astype(_BF16), wb_ref[...], preferred_element_type=_F32)
    merged = jax.nn.sigmoid(ga_ref[...]) * pa + jax.nn.sigmoid(gb_ref[...]) * pb
    mo = jnp.dot(merged.astype(_BF16), wo_ref[...], preferred_element_type=_F32)
    out_ref[...] = x_ref[...] + _rms(mo, g_ref[...])


def _merge(x, oa, dil_outs, dil_lses, ga, gb, w_proj_a, w_proj_b, w_out, g_post):
    n, d = x.shape
    tm = min(TOK_TM, n)
    row = lambda width: pl.BlockSpec((tm, width), lambda i: (i, 0))
    slabs = pl.BlockSpec((DIL_GROUP_W // LANES, tm, LANES), lambda i: (0, i, 0))
    return pl.pallas_call(
        _merge_body,
        grid=(n // tm,),
        in_specs=[row(d), row(DA_V_W)] + [slabs] * 6 + [row(d), row(d),
                  _resident(w_proj_a.shape), _resident(w_proj_b.shape), _resident(w_out.shape),
                  _resident((1, d))],
        out_specs=row(d),
        out_shape=jax.ShapeDtypeStruct((n, d), _F32),
        compiler_params=_params(1),
        name="merge",
    )(x, oa, *dil_outs, *dil_lses, ga, gb, w_proj_a.astype(_BF16), w_proj_b.astype(_BF16),
      w_out.astype(_BF16), g_post.reshape(1, d))


def _rope_patterns(positions):
    half = ROT_DIM // 2
    inv = ROPE_THETA ** (-(jnp.arange(0, ROT_DIM, 2, dtype=_F32) / ROT_DIM))
    ang = positions.astype(_F32).reshape(-1, 1) * inv
    cos, sin = jnp.cos(ang), jnp.sin(ang)
    n = ang.shape[0]
    rest = DA_HEAD_DIM - ROT_DIM
    reps = LANES // DA_HEAD_DIM
    cos_p = jnp.concatenate([cos, cos, jnp.ones((n, rest), _F32)], axis=1)
    sa_p = jnp.concatenate([jnp.zeros((n, half), _F32), sin, jnp.zeros((n, rest), _F32)], axis=1)
    sb_p = jnp.concatenate([-sin, jnp.zeros((n, half + rest), _F32)], axis=1)
    return tuple(jnp.tile(t, (1, reps)) for t in (cos_p, sa_p, sb_p))


def kernel(x, positions, w_in, lambda_q1, lambda_k1, lambda_q2, lambda_k2, g_subln, w_proj_a, w_proj_b, w_out, w_gu1, w_down1, w_gu2, w_down2, g_pre_ffn1, g_post_ffn1, g_pre_mix, g_post_mix, g_pre_ffn2, g_post_ffn2):
    b, s, d = x.shape
    n = b * s
    depth = w_in.shape[0]
    rope = _rope_patterns(positions)
    xf = x.reshape(n, d)
    for l in range(depth):
        lambda_init = 0.8 - 0.6 * math.exp(-0.3 * l)
        xf = _ffn(xf, g_pre_ffn1[l], w_gu1[l], w_down1[l], g_post_ffn1[l])

        q, k, vt, qd, kd, vd, ga, gb = _inproj(xf, g_pre_mix[l], w_in[l], rope, b, s)
        lam_vecs = jnp.stack([lambda_q1[l], lambda_k1[l], lambda_q2[l], lambda_k2[l]], axis=0)
        oa = _diffattn(q.reshape(b, s, DA_QK_W), k.reshape(b, s, DA_QK_W), vt, lam_vecs,
                       g_subln[l], lambda_init)

        dil_outs, dil_lses = [], []
        for gi, (win, dil) in enumerate(DIL_PAIRS):
            o_g, lse_g = _dilattn(qd, kd, vd, gi, dil, win // (2 * dil), s)
            dil_outs.append(o_g)
            dil_lses.append(lse_g)

        xf = _merge(xf, oa.reshape(n, DA_V_W), dil_outs, dil_lses, ga, gb,
                    w_proj_a[l], w_proj_b[l], w_out[l], g_post_mix[l])
        xf = _ffn(xf, g_pre_ffn2[l], w_gu2[l], w_down2[l], g_post_ffn2[l])
    return xf.reshape(b, s, d)
```

```python
import functools
import math

import jax
import jax.numpy as jnp
from jax import lax
from jax.experimental import pallas as pl
from jax.experimental.pallas import tpu as pltpu

D_MODEL = 1024
DA_HEADS = 8
DA_HEAD_DIM = 64
DA_V_DIM = 2 * DA_HEAD_DIM
DA_QK_W = DA_HEADS * 2 * DA_HEAD_DIM
DA_V_W = DA_HEADS * DA_V_DIM
DIL_PAIRS = ((128, 1), (512, 4), (2048, 16))
DIL_HEADS_PER_GROUP = 4
DIL_HEAD_DIM = 64
DIL_GROUP_W = DIL_HEADS_PER_GROUP * DIL_HEAD_DIM
DIL_W = DIL_GROUP_W * len(DIL_PAIRS)
ROPE_THETA = 500000.0
ROT_DIM = 16
D_FF = 2816
EPS = 1e-6
NEG = -1e30

LANES = 128
BF16_ROWS = 16
VMEM_LIMIT = 56 * 1024 * 1024

FFN_TM = 512
FFN_CHUNKS = 2
TOK_TM = 512
ATT_TQ = 512
ATT_TK = 1024
KEY_CHUNK = 256
TILES_PER_TRIP = 4
DIL_T = 2048
DIL_UNROLL = 4

_BF16 = jnp.bfloat16
_F32 = jnp.float32


def _params(n_axes):
    return pltpu.CompilerParams(dimension_semantics=("arbitrary",) * n_axes,
                                vmem_limit_bytes=VMEM_LIMIT)


def _resident(shape):
    zeros = (0,) * len(shape)
    return pl.BlockSpec(shape, lambda *_: zeros, pipeline_mode=pl.Buffered(1))


def _rms(x, g):
    ms = jnp.mean(x * x, axis=-1, keepdims=True)
    return x * lax.rsqrt(ms + EPS) * g


def _ffn_body(x_ref, gpre_ref, wgu_ref, wd_ref, gpost_ref, o_ref):
    rows = x_ref.shape[0] // FFN_CHUNKS
    for i in range(FFN_CHUNKS):
        x = x_ref[i * rows:(i + 1) * rows, :]
        h = _rms(x, gpre_ref[...]).astype(_BF16)
        gu = jnp.dot(h, wgu_ref[...], preferred_element_type=_F32)
        g = gu[:, :D_FF]
        u = gu[:, D_FF:]
        a = (g * jax.nn.sigmoid(g) * u).astype(_BF16)
        y = jnp.dot(a, wd_ref[...], preferred_element_type=_F32)
        o_ref[i * rows:(i + 1) * rows, :] = x + 0.5 * _rms(y, gpost_ref[...])


def _ffn(x, g_pre, w_gu, w_down, g_post):
    n, d = x.shape
    tm = min(FFN_TM, n)
    row = pl.BlockSpec((tm, d), lambda i: (i, 0))
    return pl.pallas_call(
        _ffn_body,
        grid=(n // tm,),
        in_specs=[row, _resident((1, d)), _resident(w_gu.shape), _resident(w_down.shape),
                  _resident((1, d))],
        out_specs=row,
        out_shape=jax.ShapeDtypeStruct((n, d), _F32),
        compiler_params=_params(1),
        name="ffn",
    )(x, g_pre.reshape(1, d), w_gu.astype(_BF16), w_down.astype(_BF16), g_post.reshape(1, d))


def _inproj_body(x_ref, g_ref, cos_ref, sa_ref, sb_ref, w_ref, wvt_ref,
                 q_ref, k_ref, vt_ref, qd_ref, kd_ref, vd_ref, ga_ref, gb_ref, *, q_scale, qd_scale):
    h = _rms(x_ref[...], g_ref[...]).astype(_BF16)
    cos, sa, sb = cos_ref[...], sa_ref[...], sb_ref[...]

    def proj(start, width):
        return jnp.dot(h, w_ref[:, start:start + width], preferred_element_type=_F32)

    def rope(zj):
        return (zj * cos + pltpu.roll(zj, ROT_DIM // 2, 1) * sa
                + pltpu.roll(zj, LANES - ROT_DIM // 2, 1) * sb)

    def rope_store(z, out_ref, scale):
        for j in range(z.shape[1] // LANES):
            r = rope(z[:, j * LANES:(j + 1) * LANES]) * scale
            out_ref[:, j * LANES:(j + 1) * LANES] = r.astype(out_ref.dtype)

    def slab_store(z, out_ref, fn):
        for j in range(z.shape[1] // LANES):
            out_ref[j] = fn(z[:, j * LANES:(j + 1) * LANES])

    c = 0
    rope_store(proj(c, DA_QK_W), q_ref, q_scale); c += DA_QK_W
    rope_store(proj(c, DA_QK_W), k_ref, 1.0); c += DA_QK_W
    vt_ref[...] = lax.dot_general(wvt_ref[...], h, (((1,), (1,)), ((), ())),
                                  preferred_element_type=_F32).astype(vt_ref.dtype)
    c += DA_V_W
    slab_store(proj(c, DIL_W), qd_ref, lambda zj: rope(zj) * qd_scale); c += DIL_W
    slab_store(proj(c, DIL_W), kd_ref, rope); c += DIL_W
    slab_store(proj(c, DIL_W), vd_ref, lambda zj: zj); c += DIL_W
    ga_ref[...] = proj(c, D_MODEL); c += D_MODEL
    gb_ref[...] = proj(c, D_MODEL)


def _inproj(x, g_pre, w_in, rope, batch, seq):
    n, d = x.shape
    tm = min(TOK_TM, seq)
    ns = seq // tm
    tk = min(ATT_TK, seq)
    per_key_tile = tk // tm
    cos, sa, sb = rope
    w = w_in.astype(_BF16)
    v0 = 2 * DA_QK_W
    wvt = w[:, v0:v0 + DA_V_W].T
    row = lambda width: pl.BlockSpec((tm, width), lambda i: (i, 0))
    nslab = DIL_W // LANES
    slabs = pl.BlockSpec((nslab, tm, LANES), lambda i: (0, i, 0))
    body = functools.partial(
        _inproj_body,
        q_scale=math.log2(math.e) / math.sqrt(DA_HEAD_DIM),
        qd_scale=1.0 / math.sqrt(DIL_HEAD_DIM))
    return pl.pallas_call(
        body,
        grid=(n // tm,),
        in_specs=[row(d), _resident((1, d)), row(LANES), row(LANES), row(LANES),
                  _resident(w.shape), _resident(wvt.shape)],
        out_specs=[row(DA_QK_W), row(DA_QK_W),
                   pl.BlockSpec((None, None, DA_V_W, tm),
                                lambda i: (i // ns, (i % ns) // per_key_tile, 0, i % per_key_tile)),
                   slabs, slabs, slabs, row(d), row(d)],
        out_shape=[jax.ShapeDtypeStruct((n, DA_QK_W), _BF16),
                   jax.ShapeDtypeStruct((n, DA_QK_W), _BF16),
                   jax.ShapeDtypeStruct((batch, seq // tk, DA_V_W, tk), _BF16),
                   jax.ShapeDtypeStruct((nslab, n, LANES), _F32),
                   jax.ShapeDtypeStruct((nslab, n, LANES), _F32),
                   jax.ShapeDtypeStruct((nslab, n, LANES), _F32),
                   jax.ShapeDtypeStruct((n, d), _F32),
                   jax.ShapeDtypeStruct((n, d), _F32)],
        compiler_params=_params(1),
        name="inproj",
    )(x, g_pre.reshape(1, d), cos, sa, sb, w, wvt)


def _diffattn_body(lam_ref, gsub_ref, q_ref, k_ref, vt_ref, o_ref,
                   s_ref, mx_ref, m_ref, acc_ref, *, tq, lambda_init):
    nk, _, tk = vt_ref.shape
    total = (q_ref.shape[0] // tq) * nk
    lane = lax.broadcasted_iota(jnp.int32, (tq, DA_V_DIM), 1)
    ones_rows = (lax.broadcasted_iota(jnp.int32, (BF16_ROWS, tk), 0) == 0).astype(_BF16)
    m_ref[...] = jnp.zeros(m_ref.shape, _F32)
    acc_ref[...] = jnp.zeros(acc_ref.shape, _F32)

    def masked_q(t):
        q = q_ref[pl.ds(pl.multiple_of((t // nk) * tq, tq), tq), :]
        zero = jnp.zeros_like(q)
        return [jnp.where((lane >= c * DA_HEAD_DIM) & (lane < (c + 1) * DA_HEAD_DIM), q, zero)
                for c in range(2)]

    def score_chunk(qc, t, slot, c, kc):
        rows = pl.ds(pl.multiple_of((t % nk) * tk + kc * KEY_CHUNK, KEY_CHUNK), KEY_CHUNK)
        s = lax.dot_general(k_ref[rows, :], qc[c], (((1,), (1,)), ((), ())),
                            preferred_element_type=_F32)
        s_ref[slot, c, kc * KEY_CHUNK:(kc + 1) * KEY_CHUNK, :] = s
        return jnp.max(s, axis=0, keepdims=True)

    def scores(t, slot):
        qc = masked_q(t)
        for c in range(2):
            cm = [score_chunk(qc, t, slot, c, kc) for kc in range(tk // KEY_CHUNK)]
            mx_ref[slot, c] = functools.reduce(jnp.maximum, cm)

    def step(t, slot, may_start):
        t_next = jnp.minimum(t + 1, total - 1)
        qc = masked_q(t_next)
        vt = jnp.concatenate([vt_ref[t % nk], ones_rows], axis=0)
        m_new, alpha = [], []
        for c in range(2):
            m_old = m_ref[c]
            if may_start:
                m_old = jnp.where(t % nk == 0, NEG, m_old)
            m_new.append(jnp.maximum(m_old, mx_ref[slot, c]))
            alpha.append(jnp.exp2(m_old - m_new[c]))
            m_ref[c] = m_new[c]
        cmax, pv = [None, None], [None, None]
        for kc in range(tk // KEY_CHUNK):
            keys = slice(kc * KEY_CHUNK, (kc + 1) * KEY_CHUNK)
            for c in range(2):
                cm = score_chunk(qc, t_next, 1 - slot, c, kc)
                cmax[c] = cm if cmax[c] is None else jnp.maximum(cmax[c], cm)
                p = jnp.exp2(s_ref[slot, c, keys, :] - m_new[c]).astype(_BF16)
                d = jnp.dot(vt[:, keys], p, preferred_element_type=_F32)
                pv[c] = d if pv[c] is None else pv[c] + d
        for c in range(2):
            mx_ref[1 - slot, c] = cmax[c]
            acc_ref[c] = alpha[c] * acc_ref[c] + pv[c]

    def finalize(qi):
        lq1, lk1, lq2, lk2 = (lam_ref[i:i + 1, :] for i in range(4))
        lam = (jnp.exp(jnp.sum(lq1 * lk1, axis=1, keepdims=True))
               - jnp.exp(jnp.sum(lq2 * lk2, axis=1, keepdims=True)) + lambda_init)
        num = [acc_ref[c, :DA_V_DIM, :] for c in range(2)]
        den = [acc_ref[c, DA_V_DIM:DA_V_DIM + 1, :] for c in range(2)]
        o = num[0] / den[0] - lam * (num[1] / den[1])
        ms = jnp.mean(o * o, axis=0, keepdims=True)
        y = o * lax.rsqrt(ms + EPS) * gsub_ref[...] * (1.0 - lambda_init)
        o_ref[pl.ds(pl.multiple_of(qi * tq, tq), tq), :] = y.T.astype(o_ref.dtype)

    scores(0, 0)

    def maybe_finalize(t_next):
        @pl.when(t_next % nk == 0)
        def _():
            finalize(t_next // nk - 1)

    def trip(u, carry):
        t0 = TILES_PER_TRIP * u
        for i in range(0, TILES_PER_TRIP, 2):
            step(t0 + i, 0, True)
            step(t0 + i + 1, 1, False)
            if (i + 2) % math.gcd(nk, TILES_PER_TRIP) == 0:
                maybe_finalize(t0 + i + 2)
        return carry

    lax.fori_loop(0, total // TILES_PER_TRIP, trip, 0)


def _diffattn(q, k, vt, lam_vecs, g_subln, lambda_init):
    b, s, _ = q.shape
    tq = min(ATT_TQ, s)
    nk, tk = vt.shape[1], vt.shape[3]
    assert nk % 2 == 0 and s % tq == 0 and tk % KEY_CHUNK == 0
    assert (s // tq * nk) % TILES_PER_TRIP == 0 and TILES_PER_TRIP % 2 == 0
    body = functools.partial(_diffattn_body, tq=tq, lambda_init=lambda_init)
    head = pl.BlockSpec((None, s, DA_V_DIM), lambda bi, h: (bi, 0, h))
    return pl.pallas_call(
        body,
        grid=(b, DA_HEADS),
        in_specs=[_resident(lam_vecs.shape), _resident((DA_V_DIM, 1)), head, head,
                  pl.BlockSpec((None, nk, DA_V_DIM, tk), lambda bi, h: (bi, 0, h, 0))],
        out_specs=head,
        out_shape=jax.ShapeDtypeStruct((b, s, DA_V_W), _BF16),
        scratch_shapes=[pltpu.VMEM((2, 2, tk, tq), _F32),
                        pltpu.VMEM((2, 2, 1, tq), _F32),
                        pltpu.VMEM((2, 1, tq), _F32),
                        pltpu.VMEM((2, DA_V_DIM + BF16_ROWS, tq), _F32)],
        compiler_params=_params(2),
        name="diffattn",
    )(lam_vecs, g_subln.reshape(DA_V_DIM, 1), q, k, vt)


def _dilattn_body(q_ref, kp_ref, km_ref, kn_ref, vp_ref, vm_ref, vn_ref, o_ref, lse_ref,
                  *, dil, half, seq_len, tiles_per_seq):
    t = q_ref.shape[1]
    per_res = t // dil
    qb, win = 2 * half, 4 * half
    n_sub = per_res // qb
    nh = DIL_HEADS_PER_GROUP
    nslab = DIL_GROUP_W // LANES
    base = (pl.program_id(0) % tiles_per_seq) * per_res
    head_of_lane = lax.broadcasted_iota(jnp.int32, (qb, DIL_GROUP_W), 1) // DIL_HEAD_DIM
    r_iota = lax.broadcasted_iota(jnp.int32, (nh * qb, win), 0) % qb
    j = lax.broadcasted_iota(jnp.int32, (nh * qb, win), 1)
    band_bias = jnp.where((j >= r_iota) & (j <= r_iota + 2 * half), 0.0, NEG).astype(_F32)
    j_row = lax.broadcasted_iota(jnp.int32, (1, win), 1)

    def rows(start, count):
        return pl.ds(start, count, stride=dil) if dil > 1 else pl.ds(start, count)

    def gather(ref, start, count):
        return jnp.concatenate([ref[sl, rows(start, count), :] for sl in range(nslab)],
                               axis=1).astype(_BF16)

    def window(prev_ref, main_ref, next_ref, res, i, row0):
        if n_sub == 1:
            lo, hi = gather(prev_ref, res, half), gather(next_ref, res, half)
        else:
            lo_main = gather(main_ref, jnp.maximum(row0 - half * dil, res), half)
            lo = jnp.where(i == 0, gather(prev_ref, res, half), lo_main)
            hi_main = gather(main_ref,
                             jnp.minimum(row0 + qb * dil, (per_res - half) * dil + res), half)
            hi = jnp.where(i == n_sub - 1, gather(next_ref, res, half), hi_main)
        return jnp.concatenate([lo, gather(main_ref, row0, qb), hi], axis=0)

    def block(idx, carry):
        res, i = idx // n_sub, idx % n_sub
        row0 = i * (qb * dil) + res
        qi = gather(q_ref, row0, qb)
        kw = window(kp_ref, km_ref, kn_ref, res, i, row0)
        vw = window(vp_ref, vm_ref, vn_ref, res, i, row0)
        kpos = base + (i * qb - half) + j_row
        in_seq = (kpos >= 0) & (kpos < seq_len)
        zero = jnp.zeros_like(qi)
        qs = jnp.concatenate([jnp.where(head_of_lane == h, qi, zero) for h in range(nh)], axis=0)
        s = lax.dot_general(qs, kw, (((1,), (1,)), ((), ())), preferred_element_type=_F32)
        s = jnp.where(in_seq, s + band_bias, NEG)
        m = jnp.max(s, axis=1, keepdims=True)
        e = jnp.exp(s - m)
        l = jnp.sum(e, axis=1, keepdims=True)
        pv = jnp.dot(e.astype(_BF16), vw, preferred_element_type=_F32) / l
        lse = m + jnp.log(l)
        o = jnp.zeros((qb, DIL_GROUP_W), _F32)
        lse_full = jnp.zeros((qb, DIL_GROUP_W), _F32)
        for h in range(nh):
            sel = head_of_lane == h
            o = jnp.where(sel, pv[h * qb:(h + 1) * qb], o)
            lse_full = jnp.where(sel, lse[h * qb:(h + 1) * qb], lse_full)
        for sl in range(nslab):
            o_ref[sl, rows(row0, qb), :] = o[:, sl * LANES:(sl + 1) * LANES]
            lse_ref[sl, rows(row0, qb), :] = lse_full[:, sl * LANES:(sl + 1) * LANES]
        return carry

    lax.fori_loop(0, dil * n_sub, block, 0, unroll=DIL_UNROLL)


def _dilattn(qd, kd, vd, group, dil, half, seq):
    nslab_all, n, _ = qd.shape
    nslab = DIL_GROUP_W // LANES
    t = min(DIL_T, seq)
    halo = half * dil
    assert seq % t == 0 and t % (2 * half * dil) == 0 and t % halo == 0 and half % 8 == 0
    tiles_per_seq = seq // t
    per_tile = t // halo
    last = n // halo - 1
    main = pl.BlockSpec((nslab, t, LANES), lambda i: (group, i, 0))
    prev = pl.BlockSpec((nslab, halo, LANES),
                        lambda i: (group, jnp.maximum(i * per_tile - 1, 0), 0))
    nxt = pl.BlockSpec((nslab, halo, LANES),
                       lambda i: (group, jnp.minimum((i + 1) * per_tile, last), 0))
    out = pl.BlockSpec((nslab, t, LANES), lambda i: (0, i, 0))
    body = functools.partial(_dilattn_body, dil=dil, half=half, seq_len=seq // dil,
                             tiles_per_seq=tiles_per_seq)
    return pl.pallas_call(
        body,
        grid=(n // t,),
        in_specs=[main, prev, main, nxt, prev, main, nxt],
        out_specs=[out, out],
        out_shape=[jax.ShapeDtypeStruct((nslab, n, LANES), _F32)] * 2,
        compiler_params=_params(1),
        name=f"dilattn_d{dil}",
    )(qd, kd, kd, kd, vd, vd, vd)


def _merge_body(x_ref, oa_ref, o0_ref, o1_ref, o2_ref, l0_ref, l1_ref, l2_ref, ga_ref, gb_ref,
                wa_ref, wb_ref, wo_ref, g_ref, out_ref):
    od_slabs = []
    for sl in range(DIL_GROUP_W // LANES):
        lses = (l0_ref[sl], l1_ref[sl], l2_ref[sl])
        outs = (o0_ref[sl], o1_ref[sl], o2_ref[sl])
        m = jnp.maximum(jnp.maximum(lses[0], lses[1]), lses[2])
        es = [jnp.exp(l - m) for l in lses]
        den = es[0] + es[1] + es[2]
        od_slabs.append((es[0] / den) * outs[0] + (es[1] / den) * outs[1]
                        + (es[2] / den) * outs[2])
    od = jnp.concatenate(od_slabs, axis=1)
    pa = jnp.dot(oa_ref[...], wa_ref[...], preferred_element_type=_F32)
    pb = jnp.dot(od.astype(_BF16), wb_ref[...], preferred_element_type=_F32)
    merged = jax.nn.sigmoid(ga_ref[...]) * pa + jax.nn.sigmoid(gb_ref[...]) * pb
    mo = jnp.dot(merged.astype(_BF16), wo_ref[...], preferred_element_type=_F32)
    out_ref[...] = x_ref[...] + _rms(mo, g_ref[...])


def _merge(x, oa, dil_outs, dil_lses, ga, gb, w_proj_a, w_proj_b, w_out, g_post):
    n, d = x.shape
    tm = min(TOK_TM, n)
    row = lambda width: pl.BlockSpec((tm, width), lambda i: (i, 0))
    slabs = pl.BlockSpec((DIL_GROUP_W // LANES, tm, LANES), lambda i: (0, i, 0))
    return pl.pallas_call(
        _merge_body,
        grid=(n // tm,),
        in_specs=[row(d), row(DA_V_W)] + [slabs] * 6 + [row(d), row(d),
                  _resident(w_proj_a.shape), _resident(w_proj_b.shape), _resident(w_out.shape),
                  _resident((1, d))],
        out_specs=row(d),
        out_shape=jax.ShapeDtypeStruct((n, d), _F32),
        compiler_params=_params(1),
        name="merge",
    )(x, oa, *dil_outs, *dil_lses, ga, gb, w_proj_a.astype(_BF16), w_proj_b.astype(_BF16),
      w_out.astype(_BF16), g_post.reshape(1, d))


def _rope_patterns(positions):
    half = ROT_DIM // 2
    inv = ROPE_THETA ** (-(jnp.arange(0, ROT_DIM, 2, dtype=_F32) / ROT_DIM))
    ang = positions.astype(_F32).reshape(-1, 1) * inv
    cos, sin = jnp.cos(ang), jnp.sin(ang)
    n = ang.shape[0]
    rest = DA_HEAD_DIM - ROT_DIM
    reps = LANES // DA_HEAD_DIM
    cos_p = jnp.concatenate([cos, cos, jnp.ones((n, rest), _F32)], axis=1)
    sa_p = jnp.concatenate([jnp.zeros((n, half), _F32), sin, jnp.zeros((n, rest), _F32)], axis=1)
    sb_p = jnp.concatenate([-sin, jnp.zeros((n, half + rest), _F32)], axis=1)
    return tuple(jnp.tile(t, (1, reps)) for t in (cos_p, sa_p, sb_p))


def kernel(x, positions, w_in, lambda_q1, lambda_k1, lambda_q2, lambda_k2, g_subln, w_proj_a, w_proj_b, w_out, w_gu1, w_down1, w_gu2, w_down2, g_pre_ffn1, g_post_ffn1, g_pre_mix, g_post_mix, g_pre_ffn2, g_post_ffn2):
    b, s, d = x.shape
    n = b * s
    depth = w_in.shape[0]
    rope = _rope_patterns(positions)
    xf = x.reshape(n, d)
    for l in range(depth):
        lambda_init = 0.8 - 0.6 * math.exp(-0.3 * l)
        xf = _ffn(xf, g_pre_ffn1[l], w_gu1[l], w_down1[l], g_post_ffn1[l])

        q, k, vt, qd, kd, vd, ga, gb = _inproj(xf, g_pre_mix[l], w_in[l], rope, b, s)
        lam_vecs = jnp.stack([lambda_q1[l], lambda_k1[l], lambda_q2[l], lambda_k2[l]], axis=0)
        oa = _diffattn(q.reshape(b, s, DA_QK_W), k.reshape(b, s, DA_QK_W), vt, lam_vecs,
                       g_subln[l], lambda_init)

        dil_outs, dil_lses = [], []
        for gi, (win, dil) in enumerate(DIL_PAIRS):
            o_g, lse_g = _dilattn(qd, kd, vd, gi, dil, win // (2 * dil), s)
            dil_outs.append(o_g)
            dil_lses.append(lse_g)

        xf = _merge(xf, oa.reshape(n, DA_V_W), dil_outs, dil_lses, ga, gb,
                    w_proj_a[l], w_proj_b[l], w_out[l], g_post_mix[l])
        xf = _ffn(xf, g_pre_ffn2[l], w_gu2[l], w_down2[l], g_post_ffn2[l])
    return xf.reshape(b, s, d)
```

```python
import functools
import math

import jax
import jax.numpy as jnp
from jax import lax
from jax.experimental import pallas as pl
from jax.experimental.pallas import tpu as pltpu

D_MODEL = 1024
DA_HEADS = 8
DA_HEAD_DIM = 64
DA_V_DIM = 2 * DA_HEAD_DIM
DA_QK_W = DA_HEADS * 2 * DA_HEAD_DIM
DA_V_W = DA_HEADS * DA_V_DIM
DIL_PAIRS = ((128, 1), (512, 4), (2048, 16))
DIL_HEADS_PER_GROUP = 4
DIL_HEAD_DIM = 64
DIL_GROUP_W = DIL_HEADS_PER_GROUP * DIL_HEAD_DIM
DIL_W = DIL_GROUP_W * len(DIL_PAIRS)
ROPE_THETA = 500000.0
ROT_DIM = 16
D_FF = 2816
EPS = 1e-6
NEG = -1e30

LANES = 128
BF16_ROWS = 16
VMEM_LIMIT = 56 * 1024 * 1024

FFN_TM = 512
FFN_CHUNKS = 2
TOK_TM = 512
ATT_TQ = 512
ATT_TK = 1024
KEY_CHUNK = 256
DIL_T = 2048
DIL_UNROLL = 16

_BF16 = jnp.bfloat16
_F32 = jnp.float32


def _params(n_axes):
    return pltpu.CompilerParams(dimension_semantics=("arbitrary",) * n_axes,
                                vmem_limit_bytes=VMEM_LIMIT)


def _resident(shape):
    zeros = (0,) * len(shape)
    return pl.BlockSpec(shape, lambda *_: zeros, pipeline_mode=pl.Buffered(1))


def _rms(x, g):
    ms = jnp.mean(x * x, axis=-1, keepdims=True)
    return x * lax.rsqrt(ms + EPS) * g


def _ffn_body(x_ref, gpre_ref, wgu_ref, wd_ref, gpost_ref, o_ref):
    rows = x_ref.shape[0] // FFN_CHUNKS
    for i in range(FFN_CHUNKS):
        x = x_ref[i * rows:(i + 1) * rows, :]
        h = _rms(x, gpre_ref[...]).astype(_BF16)
        gu = jnp.dot(h, wgu_ref[...], preferred_element_type=_F32)
        g = gu[:, :D_FF]
        u = gu[:, D_FF:]
        a = (g * jax.nn.sigmoid(g) * u).astype(_BF16)
        y = jnp.dot(a, wd_ref[...], preferred_element_type=_F32)
        o_ref[i * rows:(i + 1) * rows, :] = x + 0.5 * _rms(y, gpost_ref[...])


def _ffn(x, g_pre, w_gu, w_down, g_post):
    n, d = x.shape
    tm = min(FFN_TM, n)
    row = pl.BlockSpec((tm, d), lambda i: (i, 0))
    return pl.pallas_call(
        _ffn_body,
        grid=(n // tm,),
        in_specs=[row, _resident((1, d)), _resident(w_gu.shape), _resident(w_down.shape),
                  _resident((1, d))],
        out_specs=row,
        out_shape=jax.ShapeDtypeStruct((n, d), _F32),
        compiler_params=_params(1),
        name="ffn",
    )(x, g_pre.reshape(1, d), w_gu.astype(_BF16), w_down.astype(_BF16), g_post.reshape(1, d))


def _inproj_body(x_ref, g_ref, cos_ref, sa_ref, sb_ref, w_ref, wvt_ref,
                 q_ref, k_ref, vt_ref, qd_ref, kd_ref, vd_ref, ga_ref, gb_ref, *, q_scale, qd_scale):
    h = _rms(x_ref[...], g_ref[...]).astype(_BF16)
    cos, sa, sb = cos_ref[...], sa_ref[...], sb_ref[...]

    def proj(start, width):
        return jnp.dot(h, w_ref[:, start:start + width], preferred_element_type=_F32)

    def rope(zj):
        return (zj * cos + pltpu.roll(zj, ROT_DIM // 2, 1) * sa
                + pltpu.roll(zj, LANES - ROT_DIM // 2, 1) * sb)

    def rope_store(z, out_ref, scale):
        for j in range(z.shape[1] // LANES):
            r = rope(z[:, j * LANES:(j + 1) * LANES]) * scale
            out_ref[:, j * LANES:(j + 1) * LANES] = r.astype(out_ref.dtype)

    def slab_store(z, out_ref, fn):
        for j in range(z.shape[1] // LANES):
            out_ref[j] = fn(z[:, j * LANES:(j + 1) * LANES])

    c = 0
    rope_store(proj(c, DA_QK_W), q_ref, q_scale); c += DA_QK_W
    rope_store(proj(c, DA_QK_W), k_ref, 1.0); c += DA_QK_W
    vt_ref[...] = lax.dot_general(wvt_ref[...], h, (((1,), (1,)), ((), ())),
                                  preferred_element_type=_F32).astype(vt_ref.dtype)
    c += DA_V_W
    slab_store(proj(c, DIL_W), qd_ref, lambda zj: rope(zj) * qd_scale); c += DIL_W
    slab_store(proj(c, DIL_W), kd_ref, rope); c += DIL_W
    slab_store(proj(c, DIL_W), vd_ref, lambda zj: zj); c += DIL_W
    ga_ref[...] = proj(c, D_MODEL); c += D_MODEL
    gb_ref[...] = proj(c, D_MODEL)


def _inproj(x, g_pre, w_in, rope, batch, seq):
    n, d = x.shape
    tm = min(TOK_TM, seq)
    ns = seq // tm
    tk = min(ATT_TK, seq)
    per_key_tile = tk // tm
    cos, sa, sb = rope
    w = w_in.astype(_BF16)
    v0 = 2 * DA_QK_W
    wvt = w[:, v0:v0 + DA_V_W].T
    row = lambda width: pl.BlockSpec((tm, width), lambda i: (i, 0))
    nslab = DIL_W // LANES
    slabs = pl.BlockSpec((nslab, tm, LANES), lambda i: (0, i, 0))
    body = functools.partial(
        _inproj_body,
        q_scale=math.log2(math.e) / math.sqrt(DA_HEAD_DIM),
        qd_scale=1.0 / math.sqrt(DIL_HEAD_DIM))
    return pl.pallas_call(
        body,
        grid=(n // tm,),
        in_specs=[row(d), _resident((1, d)), row(LANES), row(LANES), row(LANES),
                  _resident(w.shape), _resident(wvt.shape)],
        out_specs=[row(DA_QK_W), row(DA_QK_W),
                   pl.BlockSpec((None, None, DA_V_W, tm),
                                lambda i: (i // ns, (i % ns) // per_key_tile, 0, i % per_key_tile)),
                   slabs, slabs, slabs, row(d), row(d)],
        out_shape=[jax.ShapeDtypeStruct((n, DA_QK_W), _BF16),
                   jax.ShapeDtypeStruct((n, DA_QK_W), _BF16),
                   jax.ShapeDtypeStruct((batch, seq // tk, DA_V_W, tk), _BF16),
                   jax.ShapeDtypeStruct((nslab, n, LANES), _F32),
                   jax.ShapeDtypeStruct((nslab, n, LANES), _F32),
                   jax.ShapeDtypeStruct((nslab, n, LANES), _F32),
                   jax.ShapeDtypeStruct((n, d), _F32),
                   jax.ShapeDtypeStruct((n, d), _F32)],
        compiler_params=_params(1),
        name="inproj",
    )(x, g_pre.reshape(1, d), cos, sa, sb, w, wvt)


def _diffattn_body(lam_ref, gsub_ref, q_ref, k_ref, vt_ref, o_ref,
                   s_ref, mx_ref, m_ref, acc_ref, fin_ref, *, tq, lambda_init):
    nk, _, tk = vt_ref.shape
    total = (q_ref.shape[0] // tq) * nk
    lane = lax.broadcasted_iota(jnp.int32, (tq, DA_V_DIM), 1)
    ones_rows = (lax.broadcasted_iota(jnp.int32, (BF16_ROWS, tk), 0) == 0).astype(_BF16)
    fin_ref[...] = jnp.ones(fin_ref.shape, _F32)

    def masked_q(t):
        q = q_ref[pl.ds(pl.multiple_of((t // nk) * tq, tq), tq), :]
        zero = jnp.zeros_like(q)
        return [jnp.where((lane >= c * DA_HEAD_DIM) & (lane < (c + 1) * DA_HEAD_DIM), q, zero)
                for c in range(2)]

    def score_chunk(qc, t, slot, c, kc):
        rows = pl.ds(pl.multiple_of((t % nk) * tk + kc * KEY_CHUNK, KEY_CHUNK), KEY_CHUNK)
        s = lax.dot_general(k_ref[rows, :], qc[c], (((1,), (1,)), ((), ())),
                            preferred_element_type=_F32)
        s_ref[slot, c, kc * KEY_CHUNK:(kc + 1) * KEY_CHUNK, :] = s
        return jnp.max(s, axis=0, keepdims=True)

    def scores(t, slot):
        qc = masked_q(t)
        for c in range(2):
            cm = [score_chunk(qc, t, slot, c, kc) for kc in range(tk // KEY_CHUNK)]
            mx_ref[slot, c] = functools.reduce(jnp.maximum, cm)

    def step(t, j, slot):
        t_next = jnp.minimum(t + 1, total - 1)
        qc = masked_q(t_next)
        vt = jnp.concatenate([vt_ref[j], ones_rows], axis=0)
        m_new, alpha = [], []
        for c in range(2):
            if j == 0:
                m_new.append(mx_ref[slot, c])
                alpha.append(None)
            else:
                m_old = m_ref[c]
                m_new.append(jnp.maximum(m_old, mx_ref[slot, c]))
                alpha.append(jnp.exp2(m_old - m_new[c]))
            m_ref[c] = m_new[c]
        cmax, pv = [None, None], [None, None]
        for kc in range(tk // KEY_CHUNK):
            keys = slice(kc * KEY_CHUNK, (kc + 1) * KEY_CHUNK)
            for c in range(2):
                cm = score_chunk(qc, t_next, 1 - slot, c, kc)
                cmax[c] = cm if cmax[c] is None else jnp.maximum(cmax[c], cm)
                p = jnp.exp2(s_ref[slot, c, keys, :] - m_new[c]).astype(_BF16)
                d = jnp.dot(vt[:, keys], p, preferred_element_type=_F32)
                pv[c] = d if pv[c] is None else pv[c] + d
        dst = fin_ref if j == nk - 1 else acc_ref
        for c in range(2):
            mx_ref[1 - slot, c] = cmax[c]
            dst[c] = pv[c] if j == 0 else alpha[c] * acc_ref[c] + pv[c]

    def finalize(qi):
        lq1, lk1, lq2, lk2 = (lam_ref[i:i + 1, :] for i in range(4))
        lam = (jnp.exp(jnp.sum(lq1 * lk1, axis=1, keepdims=True))
               - jnp.exp(jnp.sum(lq2 * lk2, axis=1, keepdims=True)) + lambda_init)
        num = [fin_ref[c, :DA_V_DIM, :] for c in range(2)]
        den = [fin_ref[c, DA_V_DIM:DA_V_DIM + 1, :] for c in range(2)]
        o = num[0] / den[0] - lam * (num[1] / den[1])
        ms = jnp.mean(o * o, axis=0, keepdims=True)
        y = o * lax.rsqrt(ms + EPS) * gsub_ref[...] * (1.0 - lambda_init)
        o_ref[pl.ds(pl.multiple_of(qi * tq, tq), tq), :] = y.T.astype(o_ref.dtype)

    scores(0, 0)

    def trip(qi, carry):
        finalize(jnp.maximum(qi - 1, 0))
        for j in range(nk):
            step(qi * nk + j, j, j % 2)
        return carry

    lax.fori_loop(0, total // nk, trip, 0)
    finalize(total // nk - 1)


def _diffattn(q, k, vt, lam_vecs, g_subln, lambda_init):
    b, s, _ = q.shape
    tq = min(ATT_TQ, s)
    nk, tk = vt.shape[1], vt.shape[3]
    assert nk % 2 == 0 and s % tq == 0 and tk % KEY_CHUNK == 0
    body = functools.partial(_diffattn_body, tq=tq, lambda_init=lambda_init)
    head = pl.BlockSpec((None, s, DA_V_DIM), lambda bi, h: (bi, 0, h))
    return pl.pallas_call(
        body,
        grid=(b, DA_HEADS),
        in_specs=[_resident(lam_vecs.shape), _resident((DA_V_DIM, 1)), head, head,
                  pl.BlockSpec((None, nk, DA_V_DIM, tk), lambda bi, h: (bi, 0, h, 0))],
        out_specs=head,
        out_shape=jax.ShapeDtypeStruct((b, s, DA_V_W), _BF16),
        scratch_shapes=[pltpu.VMEM((2, 2, tk, tq), _F32),
                        pltpu.VMEM((2, 2, 1, tq), _F32),
                        pltpu.VMEM((2, 1, tq), _F32),
                        pltpu.VMEM((2, DA_V_DIM + BF16_ROWS, tq), _F32),
                        pltpu.VMEM((2, DA_V_DIM + BF16_ROWS, tq), _F32)],
        compiler_params=_params(2),
        name="diffattn",
    )(lam_vecs, g_subln.reshape(DA_V_DIM, 1), q, k, vt)


def _dilattn_body(q_ref, kp_ref, km_ref, kn_ref, vp_ref, vm_ref, vn_ref, o_ref, lse_ref,
                  *, dil, half, seq_len, tiles_per_seq):
    t = q_ref.shape[1]
    per_res = t // dil
    qb, win = 2 * half, 4 * half
    n_sub = per_res // qb
    nh = DIL_HEADS_PER_GROUP
    nslab = DIL_GROUP_W // LANES
    base = (pl.program_id(0) % tiles_per_seq) * per_res
    head_of_lane = lax.broadcasted_iota(jnp.int32, (qb, DIL_GROUP_W), 1) // DIL_HEAD_DIM
    r_iota = lax.broadcasted_iota(jnp.int32, (nh * qb, win), 0) % qb
    j = lax.broadcasted_iota(jnp.int32, (nh * qb, win), 1)
    band_bias = jnp.where((j >= r_iota) & (j <= r_iota + 2 * half), 0.0, NEG).astype(_F32)
    j_row = lax.broadcasted_iota(jnp.int32, (1, win), 1)

    def rows(start, count):
        return pl.ds(start, count, stride=dil) if dil > 1 else pl.ds(start, count)

    def gather(ref, start, count):
        return jnp.concatenate([ref[sl, rows(start, count), :] for sl in range(nslab)],
                               axis=1).astype(_BF16)

    def window(prev_ref, main_ref, next_ref, res, i, row0):
        if n_sub == 1:
            lo, hi = gather(prev_ref, res, half), gather(next_ref, res, half)
        else:
            lo_main = gather(main_ref, jnp.maximum(row0 - half * dil, res), half)
            lo = jnp.where(i == 0, gather(prev_ref, res, half), lo_main)
            hi_main = gather(main_ref,
                             jnp.minimum(row0 + qb * dil, (per_res - half) * dil + res), half)
            hi = jnp.where(i == n_sub - 1, gather(next_ref, res, half), hi_main)
        return jnp.concatenate([lo, gather(main_ref, row0, qb), hi], axis=0)

    def block(idx, carry):
        res, i = idx // n_sub, idx % n_sub
        row0 = i * (qb * dil) + res
        qi = gather(q_ref, row0, qb)
        kw = window(kp_ref, km_ref, kn_ref, res, i, row0)
        vw = window(vp_ref, vm_ref, vn_ref, res, i, row0)
        kpos = base + (i * qb - half) + j_row
        in_seq = (kpos >= 0) & (kpos < seq_len)
        zero = jnp.zeros_like(qi)
        qs = jnp.concatenate([jnp.where(head_of_lane == h, qi, zero) for h in range(nh)], axis=0)
        s = lax.dot_general(qs, kw, (((1,), (1,)), ((), ())), preferred_element_type=_F32)
        s = jnp.where(in_seq, s + band_bias, NEG)
        m = jnp.max(s, axis=1, keepdims=True)
        e = jnp.exp(s - m)
        l = jnp.sum(e, axis=1, keepdims=True)
        pv = jnp.dot(e.astype(_BF16), vw, preferred_element_type=_F32) / l
        lse = m + jnp.log(l)
        o = jnp.zeros((qb, DIL_GROUP_W), _F32)
        lse_full = jnp.zeros((qb, DIL_GROUP_W), _F32)
        for h in range(nh):
            sel = head_of_lane == h
            o = jnp.where(sel, pv[h * qb:(h + 1) * qb], o)
            lse_full = jnp.where(sel, lse[h * qb:(h + 1) * qb], lse_full)
        for sl in range(nslab):
            o_ref[sl, rows(row0, qb), :] = o[:, sl * LANES:(sl + 1) * LANES]
            lse_ref[sl, rows(row0, qb), :] = lse_full[:, sl * LANES:(sl + 1) * LANES]
        return carry

    lax.fori_loop(0, dil * n_sub, block, 0, unroll=DIL_UNROLL)


def _dilattn(qd, kd, vd, group, dil, half, seq):
    nslab_all, n, _ = qd.shape
    nslab = DIL_GROUP_W // LANES
    t = min(DIL_T, seq)
    halo = half * dil
    assert seq % t == 0 and t % (2 * half * dil) == 0 and t % halo == 0 and half % 8 == 0
    tiles_per_seq = seq // t
    per_tile = t // halo
    last = n // halo - 1
    main = pl.BlockSpec((nslab, t, LANES), lambda i: (group, i, 0))
    prev = pl.BlockSpec((nslab, halo, LANES),
                        lambda i: (group, jnp.maximum(i * per_tile - 1, 0), 0))
    nxt = pl.BlockSpec((nslab, halo, LANES),
                       lambda i: (group, jnp.minimum((i + 1) * per_tile, last), 0))
    out = pl.BlockSpec((nslab, t, LANES), lambda i: (0, i, 0))
    body = functools.partial(_dilattn_body, dil=dil, half=half, seq_len=seq // dil,
                             tiles_per_seq=tiles_per_seq)
    return pl.pallas_call(
        body,
        grid=(n // t,),
        in_specs=[main, prev, main, nxt, prev, main, nxt],
        out_specs=[out, out],
        out_shape=[jax.ShapeDtypeStruct((nslab, n, LANES), _F32)] * 2,
        compiler_params=_params(1),
        name=f"dilattn_d{dil}",
    )(qd, kd, kd, kd, vd, vd, vd)


def _merge_body(x_ref, oa_ref, o0_ref, o1_ref, o2_ref, l0_ref, l1_ref, l2_ref, ga_ref, gb_ref,
                wa_ref, wb_ref, wo_ref, g_ref, out_ref):
    od_slabs = []
    for sl in range(DIL_GROUP_W // LANES):
        lses = (l0_ref[sl], l1_ref[sl], l2_ref[sl])
        outs = (o0_ref[sl], o1_ref[sl], o2_ref[sl])
        m = jnp.maximum(jnp.maximum(lses[0], lses[1]), lses[2])
        es = [jnp.exp(l - m) for l in lses]
        den = es[0] + es[1] + es[2]
        od_slabs.append((es[0] / den) * outs[0] + (es[1] / den) * outs[1]
                        + (es[2] / den) * outs[2])
    od = jnp.concatenate(od_slabs, axis=1)
    pa = jnp.dot(oa_ref[...], wa_ref[...], preferred_element_type=_F32)
    pb = jnp.dot(od.astype(_BF16), wb_ref[...], preferred_element_type=_F32)
    merged = jax.nn.sigmoid(ga_ref[...]) * pa + jax.nn.sigmoid(gb_ref[...]) * pb
    mo = jnp.dot(merged.astype(_BF16), wo_ref[...], preferred_element_type=_F32)
    out_ref[...] = x_ref[...] + _rms(mo, g_ref[...])


def _merge(x, oa, dil_outs, dil_lses, ga, gb, w_proj_a, w_proj_b, w_out, g_post):
    n, d = x.shape
    tm = min(TOK_TM, n)
    row = lambda width: pl.BlockSpec((tm, width), lambda i: (i, 0))
    slabs = pl.BlockSpec((DIL_GROUP_W // LANES, tm, LANES), lambda i: (0, i, 0))
    return pl.pallas_call(
        _merge_body,
        grid=(n // tm,),
        in_specs=[row(d), row(DA_V_W)] + [slabs] * 6 + [row(d), row(d),
                  _resident(w_proj_a.shape), _resident(w_proj_b.shape), _resident(w_out.shape),
                  _resident((1, d))],
        out_specs=row(d),
        out_shape=jax.ShapeDtypeStruct((n, d), _F32),
        compiler_params=_params(1),
        name="merge",
    )(x, oa, *dil_outs, *dil_lses, ga, gb, w_proj_a.astype(_BF16), w_proj_b.astype(_BF16),
      w_out.astype(_BF16), g_post.reshape(1, d))


def _rope_patterns(positions):
    half = ROT_DIM // 2
    inv = ROPE_THETA ** (-(jnp.arange(0, ROT_DIM, 2, dtype=_F32) / ROT_DIM))
    ang = positions.astype(_F32).reshape(-1, 1) * inv
    cos, sin = jnp.cos(ang), jnp.sin(ang)
    n = ang.shape[0]
    rest = DA_HEAD_DIM - ROT_DIM
    reps = LANES // DA_HEAD_DIM
    cos_p = jnp.concatenate([cos, cos, jnp.ones((n, rest), _F32)], axis=1)
    sa_p = jnp.concatenate([jnp.zeros((n, half), _F32), sin, jnp.zeros((n, rest), _F32)], axis=1)
    sb_p = jnp.concatenate([-sin, jnp.zeros((n, half + rest), _F32)], axis=1)
    return tuple(jnp.tile(t, (1, reps)) for t in (cos_p, sa_p, sb_p))


def kernel(x, positions, w_in, lambda_q1, lambda_k1, lambda_q2, lambda_k2, g_subln, w_proj_a, w_proj_b, w_out, w_gu1, w_down1, w_gu2, w_down2, g_pre_ffn1, g_post_ffn1, g_pre_mix, g_post_mix, g_pre_ffn2, g_post_ffn2):
    b, s, d = x.shape
    n = b * s
    depth = w_in.shape[0]
    rope = _rope_patterns(positions)
    xf = x.reshape(n, d)
    for l in range(depth):
        lambda_init = 0.8 - 0.6 * math.exp(-0.3 * l)
        xf = _ffn(xf, g_pre_ffn1[l], w_gu1[l], w_down1[l], g_post_ffn1[l])

        q, k, vt, qd, kd, vd, ga, gb = _inproj(xf, g_pre_mix[l], w_in[l], rope, b, s)
        lam_vecs = jnp.stack([lambda_q1[l], lambda_k1[l], lambda_q2[l], lambda_k2[l]], axis=0)
        oa = _diffattn(q.reshape(b, s, DA_QK_W), k.reshape(b, s, DA_QK_W), vt, lam_vecs,
                       g_subln[l], lambda_init)

        dil_outs, dil_lses = [], []
        for gi, (win, dil) in enumerate(DIL_PAIRS):
            o_g, lse_g = _dilattn(qd, kd, vd, gi, dil, win // (2 * dil), s)
            dil_outs.append(o_g)
            dil_lses.append(lse_g)

        xf = _merge(xf, oa.reshape(n, DA_V_W), dil_outs, dil_lses, ga, gb,
                    w_proj_a[l], w_proj_b[l], w_out[l], g_post_mix[l])
        xf = _ffn(xf, g_pre_ffn2[l], w_gu2[l], w_down2[l], g_post_ffn2[l])
    return xf.reshape(b, s, d)
```

```python
import functools
import math

import jax
import jax.numpy as jnp
import numpy as np
from jax import lax
from jax.experimental import pallas as pl
from jax.experimental.pallas import tpu as pltpu

D_MODEL = 1024
DA_HEADS = 8
DA_HEAD_DIM = 64
DA_V_DIM = 2 * DA_HEAD_DIM
DA_QK_W = DA_HEADS * 2 * DA_HEAD_DIM
DA_V_W = DA_HEADS * DA_V_DIM
DIL_PAIRS = ((128, 1), (512, 4), (2048, 16))
DIL_HEADS_PER_GROUP = 4
DIL_HEAD_DIM = 64
DIL_GROUP_W = DIL_HEADS_PER_GROUP * DIL_HEAD_DIM
DIL_W = DIL_GROUP_W * len(DIL_PAIRS)
ROPE_THETA = 500000.0
ROT_DIM = 16
D_FF = 2816
EPS = 1e-6
NEG = -1e30

LANES = 128
BF16_ROWS = 16
VMEM_LIMIT = 56 * 1024 * 1024

FFN_TM = 512
FFN_CHUNKS = 2
TOK_TM = 512
ATT_TQ = 512
ATT_TK = 1024
KEY_CHUNK = 256
DIL_T = 2048
DIL_UNROLL = 16

_BF16 = jnp.bfloat16
_F32 = jnp.float32


def _params(n_axes):
    return pltpu.CompilerParams(dimension_semantics=("arbitrary",) * n_axes,
                                vmem_limit_bytes=VMEM_LIMIT)


def _resident(shape):
    zeros = (0,) * len(shape)
    return pl.BlockSpec(shape, lambda *_: zeros, pipeline_mode=pl.Buffered(1))


def _rms(x, g):
    ms = jnp.mean(x * x, axis=-1, keepdims=True)
    return x * lax.rsqrt(ms + EPS) * g


def _ffn_body(x_ref, gpre_ref, wgu_ref, wd_ref, gpost_ref, o_ref):
    rows = x_ref.shape[0] // FFN_CHUNKS
    for i in range(FFN_CHUNKS):
        x = x_ref[i * rows:(i + 1) * rows, :]
        h = _rms(x, gpre_ref[...]).astype(_BF16)
        gu = jnp.dot(h, wgu_ref[...], preferred_element_type=_F32)
        g = gu[:, :D_FF]
        u = gu[:, D_FF:]
        a = (g * jax.nn.sigmoid(g) * u).astype(_BF16)
        y = jnp.dot(a, wd_ref[...], preferred_element_type=_F32)
        o_ref[i * rows:(i + 1) * rows, :] = x + 0.5 * _rms(y, gpost_ref[...])


def _ffn(x, g_pre, w_gu, w_down, g_post):
    n, d = x.shape
    tm = min(FFN_TM, n)
    row = pl.BlockSpec((tm, d), lambda i: (i, 0))
    return pl.pallas_call(
        _ffn_body,
        grid=(n // tm,),
        in_specs=[row, _resident((1, d)), _resident(w_gu.shape), _resident(w_down.shape),
                  _resident((1, d))],
        out_specs=row,
        out_shape=jax.ShapeDtypeStruct((n, d), _F32),
        compiler_params=_params(1),
        name="ffn",
    )(x, g_pre.reshape(1, d), w_gu.astype(_BF16), w_down.astype(_BF16), g_post.reshape(1, d))


def _inproj_body(x_ref, g_ref, cs_ref, spread_ref, w_ref, wvt_ref,
                 q_ref, k_ref, vt_ref, qd_ref, kd_ref, vd_ref, ga_ref, gb_ref, *, q_scale, qd_scale):
    h = _rms(x_ref[...], g_ref[...]).astype(_BF16)
    pats = jnp.dot(cs_ref[...], spread_ref[...], preferred_element_type=_F32)
    lane = lax.broadcasted_iota(jnp.int32, (1, LANES), 1)
    cos = pats[:, :LANES] + (lane % DA_HEAD_DIM >= ROT_DIM).astype(_F32)
    sa = pats[:, LANES:2 * LANES]
    sb = pats[:, 2 * LANES:]

    def proj(start, width):
        return jnp.dot(h, w_ref[:, start:start + width], preferred_element_type=_F32)

    def rope(zj):
        return (zj * cos + pltpu.roll(zj, ROT_DIM // 2, 1) * sa
                + pltpu.roll(zj, LANES - ROT_DIM // 2, 1) * sb)

    def rope_store(z, out_ref, scale):
        for j in range(z.shape[1] // LANES):
            r = rope(z[:, j * LANES:(j + 1) * LANES]) * scale
            out_ref[:, j * LANES:(j + 1) * LANES] = r.astype(out_ref.dtype)

    def slab_store(z, out_ref, fn):
        for j in range(z.shape[1] // LANES):
            out_ref[j] = fn(z[:, j * LANES:(j + 1) * LANES])

    c = 0
    rope_store(proj(c, DA_QK_W), q_ref, q_scale); c += DA_QK_W
    rope_store(proj(c, DA_QK_W), k_ref, 1.0); c += DA_QK_W
    vt_ref[...] = lax.dot_general(wvt_ref[...], h, (((1,), (1,)), ((), ())),
                                  preferred_element_type=_F32).astype(vt_ref.dtype)
    c += DA_V_W
    slab_store(proj(c, DIL_W), qd_ref, lambda zj: rope(zj) * qd_scale); c += DIL_W
    slab_store(proj(c, DIL_W), kd_ref, rope); c += DIL_W
    slab_store(proj(c, DIL_W), vd_ref, lambda zj: zj); c += DIL_W
    ga_ref[...] = proj(c, D_MODEL); c += D_MODEL
    gb_ref[...] = proj(c, D_MODEL)


def _inproj(x, g_pre, w_in, rope, batch, seq):
    n, d = x.shape
    tm = min(TOK_TM, seq)
    ns = seq // tm
    tk = min(ATT_TK, seq)
    per_key_tile = tk // tm
    cs, spread = rope
    w = w_in.astype(_BF16)
    v0 = 2 * DA_QK_W
    wvt = w_in[:, v0:v0 + DA_V_W].T.astype(_BF16)
    row = lambda width: pl.BlockSpec((tm, width), lambda i: (i, 0))
    nslab = DIL_W // LANES
    slabs = pl.BlockSpec((nslab, tm, LANES), lambda i: (0, i, 0))
    body = functools.partial(
        _inproj_body,
        q_scale=math.log2(math.e) / math.sqrt(DA_HEAD_DIM),
        qd_scale=1.0 / math.sqrt(DIL_HEAD_DIM))
    return pl.pallas_call(
        body,
        grid=(n // tm,),
        in_specs=[row(d), _resident((1, d)), row(cs.shape[1]), _resident(spread.shape),
                  _resident(w.shape), _resident(wvt.shape)],
        out_specs=[row(DA_QK_W), row(DA_QK_W),
                   pl.BlockSpec((None, None, DA_V_W, tm),
                                lambda i: (i // ns, (i % ns) // per_key_tile, 0, i % per_key_tile)),
                   slabs, slabs, slabs, row(d), row(d)],
        out_shape=[jax.ShapeDtypeStruct((n, DA_QK_W), _BF16),
                   jax.ShapeDtypeStruct((n, DA_QK_W), _BF16),
                   jax.ShapeDtypeStruct((batch, seq // tk, DA_V_W, tk), _BF16),
                   jax.ShapeDtypeStruct((nslab, n, LANES), _F32),
                   jax.ShapeDtypeStruct((nslab, n, LANES), _F32),
                   jax.ShapeDtypeStruct((nslab, n, LANES), _F32),
                   jax.ShapeDtypeStruct((n, d), _F32),
                   jax.ShapeDtypeStruct((n, d), _F32)],
        compiler_params=_params(1),
        name="inproj",
    )(x, g_pre.reshape(1, d), cs, spread, w, wvt)


def _diffattn_body(lam_ref, gsub_ref, q_ref, k_ref, vt_ref, o_ref,
                   s_ref, mx_ref, m_ref, acc_ref, fin_ref, *, tq, lambda_init):
    nk, _, tk = vt_ref.shape
    total = (q_ref.shape[0] // tq) * nk
    lane = lax.broadcasted_iota(jnp.int32, (tq, DA_V_DIM), 1)
    ones_rows = (lax.broadcasted_iota(jnp.int32, (BF16_ROWS, tk), 0) == 0).astype(_BF16)
    fin_ref[...] = jnp.ones(fin_ref.shape, _F32)

    def masked_q(t):
        q = q_ref[pl.ds(pl.multiple_of((t // nk) * tq, tq), tq), :]
        zero = jnp.zeros_like(q)
        return [jnp.where((lane >= c * DA_HEAD_DIM) & (lane < (c + 1) * DA_HEAD_DIM), q, zero)
                for c in range(2)]

    def score_chunk(qc, t, slot, c, kc):
        rows = pl.ds(pl.multiple_of((t % nk) * tk + kc * KEY_CHUNK, KEY_CHUNK), KEY_CHUNK)
        s = lax.dot_general(k_ref[rows, :], qc[c], (((1,), (1,)), ((), ())),
                            preferred_element_type=_F32)
        s_ref[slot, c, kc * KEY_CHUNK:(kc + 1) * KEY_CHUNK, :] = s
        return jnp.max(s, axis=0, keepdims=True)

    def scores(t, slot):
        qc = masked_q(t)
        for c in range(2):
            cm = [score_chunk(qc, t, slot, c, kc) for kc in range(tk // KEY_CHUNK)]
            mx_ref[slot, c] = functools.reduce(jnp.maximum, cm)

    def step(t, j, slot):
        t_next = jnp.minimum(t + 1, total - 1)
        qc = masked_q(t_next)
        vt = jnp.concatenate([vt_ref[j], ones_rows], axis=0)
        m_new, alpha = [], []
        for c in range(2):
            if j == 0:
                m_new.append(mx_ref[slot, c])
                alpha.append(None)
            else:
                m_old = m_ref[c]
                m_new.append(jnp.maximum(m_old, mx_ref[slot, c]))
                alpha.append(jnp.exp2(m_old - m_new[c]))
            m_ref[c] = m_new[c]
        cmax, pv = [None, None], [None, None]
        for kc in range(tk // KEY_CHUNK):
            keys = slice(kc * KEY_CHUNK, (kc + 1) * KEY_CHUNK)
            for c in range(2):
                cm = score_chunk(qc, t_next, 1 - slot, c, kc)
                cmax[c] = cm if cmax[c] is None else jnp.maximum(cmax[c], cm)
                p = jnp.exp2(s_ref[slot, c, keys, :] - m_new[c]).astype(_BF16)
                d = jnp.dot(vt[:, keys], p, preferred_element_type=_F32)
                pv[c] = d if pv[c] is None else pv[c] + d
        dst = fin_ref if j == nk - 1 else acc_ref
        for c in range(2):
            mx_ref[1 - slot, c] = cmax[c]
            dst[c] = pv[c] if j == 0 else alpha[c] * acc_ref[c] + pv[c]

    def finalize(qi):
        lq1, lk1, lq2, lk2 = (lam_ref[i:i + 1, :] for i in range(4))
        lam = (jnp.exp(jnp.sum(lq1 * lk1, axis=1, keepdims=True))
               - jnp.exp(jnp.sum(lq2 * lk2, axis=1, keepdims=True)) + lambda_init)
        num = [fin_ref[c, :DA_V_DIM, :] for c in range(2)]
        den = [fin_ref[c, DA_V_DIM:DA_V_DIM + 1, :] for c in range(2)]
        o = num[0] / den[0] - lam * (num[1] / den[1])
        ms = jnp.mean(o * o, axis=0, keepdims=True)
        y = o * lax.rsqrt(ms + EPS) * gsub_ref[...] * (1.0 - lambda_init)
        o_ref[pl.ds(pl.multiple_of(qi * tq, tq), tq), :] = y.T.astype(o_ref.dtype)

    scores(0, 0)

    def trip(qi, carry):
        finalize(jnp.maximum(qi - 1, 0))
        for j in range(nk):
            step(qi * nk + j, j, j % 2)
        return carry

    lax.fori_loop(0, total // nk, trip, 0)
    finalize(total // nk - 1)


def _diffattn(q, k, vt, lam_vecs, g_subln, lambda_init):
    b, s, _ = q.shape
    tq = min(ATT_TQ, s)
    nk, tk = vt.shape[1], vt.shape[3]
    assert nk % 2 == 0 and s % tq == 0 and tk % KEY_CHUNK == 0
    body = functools.partial(_diffattn_body, tq=tq, lambda_init=lambda_init)
    head = pl.BlockSpec((None, s, DA_V_DIM), lambda bi, h: (bi, 0, h))
    return pl.pallas_call(
        body,
        grid=(b, DA_HEADS),
        in_specs=[_resident(lam_vecs.shape), _resident((DA_V_DIM, 1)), head, head,
                  pl.BlockSpec((None, nk, DA_V_DIM, tk), lambda bi, h: (bi, 0, h, 0))],
        out_specs=head,
        out_shape=jax.ShapeDtypeStruct((b, s, DA_V_W), _BF16),
        scratch_shapes=[pltpu.VMEM((2, 2, tk, tq), _F32),
                        pltpu.VMEM((2, 2, 1, tq), _F32),
                        pltpu.VMEM((2, 1, tq), _F32),
                        pltpu.VMEM((2, DA_V_DIM + BF16_ROWS, tq), _F32),
                        pltpu.VMEM((2, DA_V_DIM + BF16_ROWS, tq), _F32)],
        compiler_params=_params(2),
        name="diffattn",
    )(lam_vecs, g_subln.reshape(DA_V_DIM, 1), q, k, vt)


def _dilattn_body(q_ref, kp_ref, km_ref, kn_ref, vp_ref, vm_ref, vn_ref, o_ref, lse_ref,
                  *, dil, half, seq_len, tiles_per_seq):
    t = q_ref.shape[1]
    per_res = t // dil
    qb, win = 2 * half, 4 * half
    n_sub = per_res // qb
    nh = DIL_HEADS_PER_GROUP
    nslab = DIL_GROUP_W // LANES
    base = (pl.program_id(0) % tiles_per_seq) * per_res
    head_of_lane = lax.broadcasted_iota(jnp.int32, (qb, DIL_GROUP_W), 1) // DIL_HEAD_DIM
    r_iota = lax.broadcasted_iota(jnp.int32, (nh * qb, win), 0) % qb
    j = lax.broadcasted_iota(jnp.int32, (nh * qb, win), 1)
    band_bias = jnp.where((j >= r_iota) & (j <= r_iota + 2 * half), 0.0, NEG).astype(_F32)
    j_row = lax.broadcasted_iota(jnp.int32, (1, win), 1)

    def rows(start, count):
        return pl.ds(start, count, stride=dil) if dil > 1 else pl.ds(start, count)

    def gather(ref, start, count):
        return jnp.concatenate([ref[sl, rows(start, count), :] for sl in range(nslab)],
                               axis=1).astype(_BF16)

    def window(prev_ref, main_ref, next_ref, res, i, row0):
        if n_sub == 1:
            lo, hi = gather(prev_ref, res, half), gather(next_ref, res, half)
        else:
            lo_main = gather(main_ref, jnp.maximum(row0 - half * dil, res), half)
            lo = jnp.where(i == 0, gather(prev_ref, res, half), lo_main)
            hi_main = gather(main_ref,
                             jnp.minimum(row0 + qb * dil, (per_res - half) * dil + res), half)
            hi = jnp.where(i == n_sub - 1, gather(next_ref, res, half), hi_main)
        return jnp.concatenate([lo, gather(main_ref, row0, qb), hi], axis=0)

    def block(idx, carry):
        res, i = idx // n_sub, idx % n_sub
        row0 = i * (qb * dil) + res
        qi = gather(q_ref, row0, qb)
        kw = window(kp_ref, km_ref, kn_ref, res, i, row0)
        vw = window(vp_ref, vm_ref, vn_ref, res, i, row0)
        kpos = base + (i * qb - half) + j_row
        in_seq = (kpos >= 0) & (kpos < seq_len)
        zero = jnp.zeros_like(qi)
        qs = jnp.concatenate([jnp.where(head_of_lane == h, qi, zero) for h in range(nh)], axis=0)
        s = lax.dot_general(qs, kw, (((1,), (1,)), ((), ())), preferred_element_type=_F32)
        s = jnp.where(in_seq, s + band_bias, NEG)
        m = jnp.max(s, axis=1, keepdims=True)
        e = jnp.exp(s - m)
        l = jnp.sum(e, axis=1, keepdims=True)
        pv = jnp.dot(e.astype(_BF16), vw, preferred_element_type=_F32) / l
        lse = m + jnp.log(l)
        o = jnp.zeros((qb, DIL_GROUP_W), _F32)
        lse_full = jnp.zeros((qb, DIL_GROUP_W), _F32)
        for h in range(nh):
            sel = head_of_lane == h
            o = jnp.where(sel, pv[h * qb:(h + 1) * qb], o)
            lse_full = jnp.where(sel, lse[h * qb:(h + 1) * qb], lse_full)
        for sl in range(nslab):
            o_ref[sl, rows(row0, qb), :] = o[:, sl * LANES:(sl + 1) * LANES]
            lse_ref[sl, rows(row0, qb), :] = lse_full[:, sl * LANES:(sl + 1) * LANES]
        return carry

    lax.fori_loop(0, dil * n_sub, block, 0, unroll=DIL_UNROLL)


def _dilattn(qd, kd, vd, group, dil, half, seq):
    nslab_all, n, _ = qd.shape
    nslab = DIL_GROUP_W // LANES
    t = min(DIL_T, seq)
    halo = half * dil
    assert seq % t == 0 and t % (2 * half * dil) == 0 and t % halo == 0 and half % 8 == 0
    tiles_per_seq = seq // t
    per_tile = t // halo
    last = n // halo - 1
    main = pl.BlockSpec((nslab, t, LANES), lambda i: (group, i, 0))
    prev = pl.BlockSpec((nslab, halo, LANES),
                        lambda i: (group, jnp.maximum(i * per_tile - 1, 0), 0))
    nxt = pl.BlockSpec((nslab, halo, LANES),
                       lambda i: (group, jnp.minimum((i + 1) * per_tile, last), 0))
    out = pl.BlockSpec((nslab, t, LANES), lambda i: (0, i, 0))
    body = functools.partial(_dilattn_body, dil=dil, half=half, seq_len=seq // dil,
                             tiles_per_seq=tiles_per_seq)
    return pl.pallas_call(
        body,
        grid=(n // t,),
        in_specs=[main, prev, main, nxt, prev, main, nxt],
        out_specs=[out, out],
        out_shape=[jax.ShapeDtypeStruct((nslab, n, LANES), _F32)] * 2,
        compiler_params=_params(1),
        name=f"dilattn_d{dil}",
    )(qd, kd, kd, kd, vd, vd, vd)


def _merge_body(x_ref, oa_ref, o0_ref, o1_ref, o2_ref, l0_ref, l1_ref, l2_ref, ga_ref, gb_ref,
                wa_ref, wb_ref, wo_ref, g_ref, out_ref):
    od_slabs = []
    for sl in range(DIL_GROUP_W // LANES):
        lses = (l0_ref[sl], l1_ref[sl], l2_ref[sl])
        outs = (o0_ref[sl], o1_ref[sl], o2_ref[sl])
        m = jnp.maximum(jnp.maximum(lses[0], lses[1]), lses[2])
        es = [jnp.exp(l - m) for l in lses]
        den = es[0] + es[1] + es[2]
        od_slabs.append((es[0] / den) * outs[0] + (es[1] / den) * outs[1]
                        + (es[2] / den) * outs[2])
    od = jnp.concatenate(od_slabs, axis=1)
    pa = jnp.dot(oa_ref[...], wa_ref[...], preferred_element_type=_F32)
    pb = jnp.dot(od.astype(_BF16), wb_ref[...], preferred_element_type=_F32)
    merged = jax.nn.sigmoid(ga_ref[...]) * pa + jax.nn.sigmoid(gb_ref[...]) * pb
    mo = jnp.dot(merged.astype(_BF16), wo_ref[...], preferred_element_type=_F32)
    out_ref[...] = x_ref[...] + _rms(mo, g_ref[...])


def _merge(x, oa, dil_outs, dil_lses, ga, gb, w_proj_a, w_proj_b, w_out, g_post):
    n, d = x.shape
    tm = min(TOK_TM, n)
    row = lambda width: pl.BlockSpec((tm, width), lambda i: (i, 0))
    slabs = pl.BlockSpec((DIL_GROUP_W // LANES, tm, LANES), lambda i: (0, i, 0))
    return pl.pallas_call(
        _merge_body,
        grid=(n // tm,),
        in_specs=[row(d), row(DA_V_W)] + [slabs] * 6 + [row(d), row(d),
                  _resident(w_proj_a.shape), _resident(w_proj_b.shape), _resident(w_out.shape),
                  _resident((1, d))],
        out_specs=row(d),
        out_shape=jax.ShapeDtypeStruct((n, d), _F32),
        compiler_params=_params(1),
        name="merge",
    )(x, oa, *dil_outs, *dil_lses, ga, gb, w_proj_a.astype(_BF16), w_proj_b.astype(_BF16),
      w_out.astype(_BF16), g_post.reshape(1, d))


def _rope_tables(positions):
    half = ROT_DIM // 2
    inv = ROPE_THETA ** (-(jnp.arange(0, ROT_DIM, 2, dtype=_F32) / ROT_DIM))
    ang = positions.astype(_F32).reshape(-1, 1) * inv
    cs = jnp.concatenate([jnp.cos(ang), jnp.sin(ang)], axis=1)
    terms = []
    for _ in range(3):
        head = lax.bitcast_convert_type(
            lax.bitcast_convert_type(cs, jnp.uint32) & jnp.uint32(0xFFFF0000), _F32)
        terms.append(head.astype(_BF16))
        cs = cs - head
    spread = np.zeros((ROT_DIM, 3 * LANES), np.float32)
    for lane in range(LANES):
        p = lane % DA_HEAD_DIM
        if p < ROT_DIM:
            spread[p % half, lane] = 1.0
        if half <= p < ROT_DIM:
            spread[p, LANES + lane] = 1.0
        if p < half:
            spread[half + p, 2 * LANES + lane] = -1.0
    return (jnp.concatenate(terms, axis=1),
            jnp.asarray(np.concatenate([spread] * 3, axis=0), dtype=_BF16))


def kernel(x, positions, w_in, lambda_q1, lambda_k1, lambda_q2, lambda_k2, g_subln, w_proj_a, w_proj_b, w_out, w_gu1, w_down1, w_gu2, w_down2, g_pre_ffn1, g_post_ffn1, g_pre_mix, g_post_mix, g_pre_ffn2, g_post_ffn2):
    b, s, d = x.shape
    n = b * s
    depth = w_in.shape[0]
    rope = _rope_tables(positions)
    xf = x.reshape(n, d)
    for l in range(depth):
        lambda_init = 0.8 - 0.6 * math.exp(-0.3 * l)
        xf = _ffn(xf, g_pre_ffn1[l], w_gu1[l], w_down1[l], g_post_ffn1[l])

        q, k, vt, qd, kd, vd, ga, gb = _inproj(xf, g_pre_mix[l], w_in[l], rope, b, s)
        lam_vecs = jnp.stack([lambda_q1[l], lambda_k1[l], lambda_q2[l], lambda_k2[l]], axis=0)
        oa = _diffattn(q.reshape(b, s, DA_QK_W), k.reshape(b, s, DA_QK_W), vt, lam_vecs,
                       g_subln[l], lambda_init)

        dil_outs, dil_lses = [], []
        for gi, (win, dil) in enumerate(DIL_PAIRS):
            o_g, lse_g = _dilattn(qd, kd, vd, gi, dil, win // (2 * dil), s)
            dil_outs.append(o_g)
            dil_lses.append(lse_g)

        xf = _merge(xf, oa.reshape(n, DA_V_W), dil_outs, dil_lses, ga, gb,
                    w_proj_a[l], w_proj_b[l], w_out[l], g_post_mix[l])
        xf = _ffn(xf, g_pre_ffn2[l], w_gu2[l], w_down2[l], g_post_ffn2[l])
    return xf.reshape(b, s, d)
```

```python
import functools
import math

import jax
import jax.numpy as jnp
import numpy as np
from jax import lax
from jax.experimental import pallas as pl
from jax.experimental.pallas import tpu as pltpu

D_MODEL = 1024
DA_HEADS = 8
DA_HEAD_DIM = 64
DA_V_DIM = 2 * DA_HEAD_DIM
DA_QK_W = DA_HEADS * 2 * DA_HEAD_DIM
DA_V_W = DA_HEADS * DA_V_DIM
DIL_PAIRS = ((128, 1), (512, 4), (2048, 16))
DIL_HEADS_PER_GROUP = 4
DIL_HEAD_DIM = 64
DIL_GROUP_W = DIL_HEADS_PER_GROUP * DIL_HEAD_DIM
DIL_W = DIL_GROUP_W * len(DIL_PAIRS)
ROPE_THETA = 500000.0
ROT_DIM = 16
D_FF = 2816
EPS = 1e-6
NEG = -1e30

LANES = 128
BF16_ROWS = 16
VMEM_LIMIT = 56 * 1024 * 1024

FFN_TM = 512
FFN_CHUNKS = 2
TOK_TM = 512
ATT_TQ = 512
ATT_TK = 1024
KEY_CHUNK = 256
DIL_T = 2048
DIL_UNROLL = 16

_BF16 = jnp.bfloat16
_F32 = jnp.float32


def _params(n_axes):
    return pltpu.CompilerParams(dimension_semantics=("arbitrary",) * n_axes,
                                vmem_limit_bytes=VMEM_LIMIT)


def _resident(shape):
    zeros = (0,) * len(shape)
    return pl.BlockSpec(shape, lambda *_: zeros, pipeline_mode=pl.Buffered(1))


def _rms(x, g):
    ms = jnp.mean(x * x, axis=-1, keepdims=True)
    return x * lax.rsqrt(ms + EPS) * g


def _ffn_body(x_ref, gpre_ref, wgu_ref, wd_ref, gpost_ref, o_ref):
    rows = x_ref.shape[0] // FFN_CHUNKS
    for i in range(FFN_CHUNKS):
        x = x_ref[i * rows:(i + 1) * rows, :]
        h = _rms(x, gpre_ref[...]).astype(_BF16)
        gu = jnp.dot(h, wgu_ref[...], preferred_element_type=_F32)
        g = gu[:, :D_FF]
        u = gu[:, D_FF:]
        a = (g * jax.nn.sigmoid(g) * u).astype(_BF16)
        y = jnp.dot(a, wd_ref[...], preferred_element_type=_F32)
        o_ref[i * rows:(i + 1) * rows, :] = x + 0.5 * _rms(y, gpost_ref[...])


def _ffn(x, g_pre, w_gu, w_down, g_post):
    n, d = x.shape
    tm = min(FFN_TM, n)
    row = pl.BlockSpec((tm, d), lambda i: (i, 0))
    return pl.pallas_call(
        _ffn_body,
        grid=(n // tm,),
        in_specs=[row, _resident((1, d)), _resident(w_gu.shape), _resident(w_down.shape),
                  _resident((1, d))],
        out_specs=row,
        out_shape=jax.ShapeDtypeStruct((n, d), _F32),
        compiler_params=_params(1),
        name="ffn",
    )(x, g_pre.reshape(1, d), w_gu.astype(_BF16), w_down.astype(_BF16), g_post.reshape(1, d))


def _inproj_body(x_ref, g_ref, cs_ref, spread_ref, w_ref, wvt_ref,
                 q_ref, k_ref, vt_ref, qd_ref, kd_ref, vd_ref, ga_ref, gb_ref, *, q_scale, qd_scale):
    h = _rms(x_ref[...], g_ref[...]).astype(_BF16)
    pats = jnp.dot(cs_ref[...], spread_ref[...], preferred_element_type=_F32)
    lane = lax.broadcasted_iota(jnp.int32, (1, LANES), 1)
    cos = pats[:, :LANES] + (lane % DA_HEAD_DIM >= ROT_DIM).astype(_F32)
    sa = pats[:, LANES:2 * LANES]
    sb = pats[:, 2 * LANES:]

    def proj(start, width):
        return jnp.dot(h, w_ref[:, start:start + width], preferred_element_type=_F32)

    def rope(zj):
        return (zj * cos + pltpu.roll(zj, ROT_DIM // 2, 1) * sa
                + pltpu.roll(zj, LANES - ROT_DIM // 2, 1) * sb)

    def rope_store(z, out_ref, scale):
        for j in range(z.shape[1] // LANES):
            r = rope(z[:, j * LANES:(j + 1) * LANES]) * scale
            out_ref[:, j * LANES:(j + 1) * LANES] = r.astype(out_ref.dtype)

    def slab_store(z, out_ref, fn):
        for j in range(z.shape[1] // LANES):
            out_ref[j] = fn(z[:, j * LANES:(j + 1) * LANES])

    c = 0
    rope_store(proj(c, DA_QK_W), q_ref, q_scale); c += DA_QK_W
    rope_store(proj(c, DA_QK_W), k_ref, 1.0); c += DA_QK_W
    vt_ref[...] = lax.dot_general(wvt_ref[...], h, (((1,), (1,)), ((), ())),
                                  preferred_element_type=_F32).astype(vt_ref.dtype)
    c += DA_V_W
    slab_store(proj(c, DIL_W), qd_ref, lambda zj: rope(zj) * qd_scale); c += DIL_W
    slab_store(proj(c, DIL_W), kd_ref, rope); c += DIL_W
    slab_store(proj(c, DIL_W), vd_ref, lambda zj: zj); c += DIL_W
    ga_ref[...] = proj(c, D_MODEL).astype(ga_ref.dtype); c += D_MODEL
    gb_ref[...] = proj(c, D_MODEL).astype(gb_ref.dtype)


def _transpose_cast_body(x_ref, o_ref):
    o_ref[...] = x_ref[...].T.astype(o_ref.dtype)


def _transpose_cast(x):
    rows, cols = x.shape
    tc = min(2 * LANES, cols)
    return pl.pallas_call(
        _transpose_cast_body,
        grid=(cols // tc,),
        in_specs=[pl.BlockSpec((rows, tc), lambda i: (0, i))],
        out_specs=pl.BlockSpec((tc, rows), lambda i: (i, 0)),
        out_shape=jax.ShapeDtypeStruct((cols, rows), _BF16),
        compiler_params=_params(1),
        name="wv_transpose",
    )(x)


def _inproj(x, g_pre, w_in, rope, batch, seq):
    n, d = x.shape
    tm = min(TOK_TM, seq)
    ns = seq // tm
    tk = min(ATT_TK, seq)
    per_key_tile = tk // tm
    cs, spread = rope
    w = w_in.astype(_BF16)
    v0 = 2 * DA_QK_W
    wvt = _transpose_cast(w_in[:, v0:v0 + DA_V_W])
    row = lambda width: pl.BlockSpec((tm, width), lambda i: (i, 0))
    nslab = DIL_W // LANES
    slabs = pl.BlockSpec((nslab, tm, LANES), lambda i: (0, i, 0))
    body = functools.partial(
        _inproj_body,
        q_scale=math.log2(math.e) / math.sqrt(DA_HEAD_DIM),
        qd_scale=1.0 / math.sqrt(DIL_HEAD_DIM))
    return pl.pallas_call(
        body,
        grid=(n // tm,),
        in_specs=[row(d), _resident((1, d)), row(cs.shape[1]), _resident(spread.shape),
                  _resident(w.shape), _resident(wvt.shape)],
        out_specs=[row(DA_QK_W), row(DA_QK_W),
                   pl.BlockSpec((None, None, DA_V_W, tm),
                                lambda i: (i // ns, (i % ns) // per_key_tile, 0, i % per_key_tile)),
                   slabs, slabs, slabs, row(d), row(d)],
        out_shape=[jax.ShapeDtypeStruct((n, DA_QK_W), _BF16),
                   jax.ShapeDtypeStruct((n, DA_QK_W), _BF16),
                   jax.ShapeDtypeStruct((batch, seq // tk, DA_V_W, tk), _BF16),
                   jax.ShapeDtypeStruct((nslab, n, LANES), _F32),
                   jax.ShapeDtypeStruct((nslab, n, LANES), _F32),
                   jax.ShapeDtypeStruct((nslab, n, LANES), _F32),
                   jax.ShapeDtypeStruct((n, d), _BF16),
                   jax.ShapeDtypeStruct((n, d), _BF16)],
        compiler_params=_params(1),
        name="inproj",
    )(x, g_pre.reshape(1, d), cs, spread, w, wvt)


def _diffattn_body(lam_ref, gsub_ref, q_ref, k_ref, vt_ref, o_ref,
                   s_ref, mx_ref, m_ref, acc_ref, fin_ref, *, tq, lambda_init):
    nk, _, tk = vt_ref.shape
    total = (q_ref.shape[0] // tq) * nk
    lane = lax.broadcasted_iota(jnp.int32, (tq, DA_V_DIM), 1)
    ones_rows = (lax.broadcasted_iota(jnp.int32, (BF16_ROWS, tk), 0) == 0).astype(_BF16)
    fin_ref[...] = jnp.ones(fin_ref.shape, _F32)

    def masked_q(t):
        q = q_ref[pl.ds(pl.multiple_of((t // nk) * tq, tq), tq), :]
        zero = jnp.zeros_like(q)
        return [jnp.where((lane >= c * DA_HEAD_DIM) & (lane < (c + 1) * DA_HEAD_DIM), q, zero)
                for c in range(2)]

    def score_chunk(qc, t, slot, c, kc):
        rows = pl.ds(pl.multiple_of((t % nk) * tk + kc * KEY_CHUNK, KEY_CHUNK), KEY_CHUNK)
        s = lax.dot_general(k_ref[rows, :], qc[c], (((1,), (1,)), ((), ())),
                            preferred_element_type=_F32)
        s_ref[slot, c, kc * KEY_CHUNK:(kc + 1) * KEY_CHUNK, :] = s
        return jnp.max(s, axis=0, keepdims=True)

    def scores(t, slot):
        qc = masked_q(t)
        for c in range(2):
            cm = [score_chunk(qc, t, slot, c, kc) for kc in range(tk // KEY_CHUNK)]
            mx_ref[slot, c] = functools.reduce(jnp.maximum, cm)

    def step(t, j, slot):
        t_next = jnp.minimum(t + 1, total - 1)
        qc = masked_q(t_next)
        vt = jnp.concatenate([vt_ref[j], ones_rows], axis=0)
        m_new, alpha = [], []
        for c in range(2):
            if j == 0:
                m_new.append(mx_ref[slot, c])
                alpha.append(None)
            else:
                m_old = m_ref[c]
                m_new.append(jnp.maximum(m_old, mx_ref[slot, c]))
                alpha.append(jnp.exp2(m_old - m_new[c]))
            m_ref[c] = m_new[c]
        cmax, pv = [None, None], [None, None]
        for kc in range(tk // KEY_CHUNK):
            keys = slice(kc * KEY_CHUNK, (kc + 1) * KEY_CHUNK)
            for c in range(2):
                cm = score_chunk(qc, t_next, 1 - slot, c, kc)
                cmax[c] = cm if cmax[c] is None else jnp.maximum(cmax[c], cm)
                p = jnp.exp2(s_ref[slot, c, keys, :] - m_new[c]).astype(_BF16)
                d = jnp.dot(vt[:, keys], p, preferred_element_type=_F32)
                pv[c] = d if pv[c] is None else pv[c] + d
        dst = fin_ref if j == nk - 1 else acc_ref
        for c in range(2):
            mx_ref[1 - slot, c] = cmax[c]
            dst[c] = pv[c] if j == 0 else alpha[c] * acc_ref[c] + pv[c]

    def finalize(qi):
        lq1, lk1, lq2, lk2 = (lam_ref[i:i + 1, :] for i in range(4))
        lam = (jnp.exp(jnp.sum(lq1 * lk1, axis=1, keepdims=True))
               - jnp.exp(jnp.sum(lq2 * lk2, axis=1, keepdims=True)) + lambda_init)
        num = [fin_ref[c, :DA_V_DIM, :] for c in range(2)]
        den = [fin_ref[c, DA_V_DIM:DA_V_DIM + 1, :] for c in range(2)]
        o = num[0] / den[0] - lam * (num[1] / den[1])
        ms = jnp.mean(o * o, axis=0, keepdims=True)
        y = o * lax.rsqrt(ms + EPS) * gsub_ref[...] * (1.0 - lambda_init)
        o_ref[pl.ds(pl.multiple_of(qi * tq, tq), tq), :] = y.T.astype(o_ref.dtype)

    scores(0, 0)

    def trip(qi, carry):
        finalize(jnp.maximum(qi - 1, 0))
        for j in range(nk):
            step(qi * nk + j, j, j % 2)
        return carry

    lax.fori_loop(0, total // nk, trip, 0)
    finalize(total // nk - 1)


def _diffattn(q, k, vt, lam_vecs, g_subln, lambda_init):
    b, s, _ = q.shape
    tq = min(ATT_TQ, s)
    nk, tk = vt.shape[1], vt.shape[3]
    assert nk % 2 == 0 and s % tq == 0 and tk % KEY_CHUNK == 0
    body = functools.partial(_diffattn_body, tq=tq, lambda_init=lambda_init)
    head = pl.BlockSpec((None, s, DA_V_DIM), lambda bi, h: (bi, 0, h))
    return pl.pallas_call(
        body,
        grid=(b, DA_HEADS),
        in_specs=[_resident(lam_vecs.shape), _resident((DA_V_DIM, 1)), head, head,
                  pl.BlockSpec((None, nk, DA_V_DIM, tk), lambda bi, h: (bi, 0, h, 0))],
        out_specs=head,
        out_shape=jax.ShapeDtypeStruct((b, s, DA_V_W), _BF16),
        scratch_shapes=[pltpu.VMEM((2, 2, tk, tq), _F32),
                        pltpu.VMEM((2, 2, 1, tq), _F32),
                        pltpu.VMEM((2, 1, tq), _F32),
                        pltpu.VMEM((2, DA_V_DIM + BF16_ROWS, tq), _F32),
                        pltpu.VMEM((2, DA_V_DIM + BF16_ROWS, tq), _F32)],
        compiler_params=_params(2),
        name="diffattn",
    )(lam_vecs, g_subln.reshape(DA_V_DIM, 1), q, k, vt)


def _dilattn_body(q_ref, kp_ref, km_ref, kn_ref, vp_ref, vm_ref, vn_ref, o_ref, lse_ref,
                  *, dil, half, seq_len, tiles_per_seq):
    t = q_ref.shape[1]
    per_res = t // dil
    qb, win = 2 * half, 4 * half
    n_sub = per_res // qb
    nh = DIL_HEADS_PER_GROUP
    nslab = DIL_GROUP_W // LANES
    base = (pl.program_id(0) % tiles_per_seq) * per_res
    head_of_lane = lax.broadcasted_iota(jnp.int32, (qb, DIL_GROUP_W), 1) // DIL_HEAD_DIM
    r_iota = lax.broadcasted_iota(jnp.int32, (nh * qb, win), 0) % qb
    j = lax.broadcasted_iota(jnp.int32, (nh * qb, win), 1)
    band_bias = jnp.where((j >= r_iota) & (j <= r_iota + 2 * half), 0.0, NEG).astype(_F32)
    j_row = lax.broadcasted_iota(jnp.int32, (1, win), 1)

    def rows(start, count):
        return pl.ds(start, count, stride=dil) if dil > 1 else pl.ds(start, count)

    def gather(ref, start, count):
        return jnp.concatenate([ref[sl, rows(start, count), :] for sl in range(nslab)],
                               axis=1).astype(_BF16)

    def window(prev_ref, main_ref, next_ref, res, i, row0):
        if n_sub == 1:
            lo, hi = gather(prev_ref, res, half), gather(next_ref, res, half)
        else:
            lo_main = gather(main_ref, jnp.maximum(row0 - half * dil, res), half)
            lo = jnp.where(i == 0, gather(prev_ref, res, half), lo_main)
            hi_main = gather(main_ref,
                             jnp.minimum(row0 + qb * dil, (per_res - half) * dil + res), half)
            hi = jnp.where(i == n_sub - 1, gather(next_ref, res, half), hi_main)
        return jnp.concatenate([lo, gather(main_ref, row0, qb), hi], axis=0)

    def block(idx, carry):
        res, i = idx // n_sub, idx % n_sub
        row0 = i * (qb * dil) + res
        qi = gather(q_ref, row0, qb)
        kw = window(kp_ref, km_ref, kn_ref, res, i, row0)
        vw = window(vp_ref, vm_ref, vn_ref, res, i, row0)
        kpos = base + (i * qb - half) + j_row
        in_seq = (kpos >= 0) & (kpos < seq_len)
        zero = jnp.zeros_like(qi)
        qs = jnp.concatenate([jnp.where(head_of_lane == h, qi, zero) for h in range(nh)], axis=0)
        s = lax.dot_general(qs, kw, (((1,), (1,)), ((), ())), preferred_element_type=_F32)
        s = jnp.where(in_seq, s + band_bias, NEG)
        m = jnp.max(s, axis=1, keepdims=True)
        e = jnp.exp(s - m)
        l = jnp.sum(e, axis=1, keepdims=True)
        pv = jnp.dot(e.astype(_BF16), vw, preferred_element_type=_F32) / l
        lse = m + jnp.log(l)
        o = jnp.zeros((qb, DIL_GROUP_W), _F32)
        lse_full = jnp.zeros((qb, DIL_GROUP_W), _F32)
        for h in range(nh):
            sel = head_of_lane == h
            o = jnp.where(sel, pv[h * qb:(h + 1) * qb], o)
            lse_full = jnp.where(sel, lse[h * qb:(h + 1) * qb], lse_full)
        for sl in range(nslab):
            o_ref[sl, rows(row0, qb), :] = o[:, sl * LANES:(sl + 1) * LANES]
            lse_ref[sl, rows(row0, qb), :] = lse_full[:, sl * LANES:(sl + 1) * LANES]
        return carry

    lax.fori_loop(0, dil * n_sub, block, 0, unroll=DIL_UNROLL)


def _dilattn(qd, kd, vd, group, dil, half, seq):
    nslab_all, n, _ = qd.shape
    nslab = DIL_GROUP_W // LANES
    t = min(DIL_T, seq)
    halo = half * dil
    assert seq % t == 0 and t % (2 * half * dil) == 0 and t % halo == 0 and half % 8 == 0
    tiles_per_seq = seq // t
    per_tile = t // halo
    last = n // halo - 1
    main = pl.BlockSpec((nslab, t, LANES), lambda i: (group, i, 0))
    prev = pl.BlockSpec((nslab, halo, LANES),
                        lambda i: (group, jnp.maximum(i * per_tile - 1, 0), 0))
    nxt = pl.BlockSpec((nslab, halo, LANES),
                       lambda i: (group, jnp.minimum((i + 1) * per_tile, last), 0))
    out = pl.BlockSpec((nslab, t, LANES), lambda i: (0, i, 0))
    body = functools.partial(_dilattn_body, dil=dil, half=half, seq_len=seq // dil,
                             tiles_per_seq=tiles_per_seq)
    return pl.pallas_call(
        body,
        grid=(n // t,),
        in_specs=[main, prev, main, nxt, prev, main, nxt],
        out_specs=[out, out],
        out_shape=[jax.ShapeDtypeStruct((nslab, n, LANES), _F32)] * 2,
        compiler_params=_params(1),
        name=f"dilattn_d{dil}",
    )(qd, kd, kd, kd, vd, vd, vd)


def _merge_body(x_ref, oa_ref, o0_ref, o1_ref, o2_ref, l0_ref, l1_ref, l2_ref, ga_ref, gb_ref,
                wa_ref, wb_ref, wo_ref, g_ref, out_ref):
    od_slabs = []
    for sl in range(DIL_GROUP_W // LANES):
        lses = (l0_ref[sl], l1_ref[sl], l2_ref[sl])
        outs = (o0_ref[sl], o1_ref[sl], o2_ref[sl])
        m = jnp.maximum(jnp.maximum(lses[0], lses[1]), lses[2])
        es = [jnp.exp(l - m) for l in lses]
        den = es[0] + es[1] + es[2]
        od_slabs.append((es[0] / den) * outs[0] + (es[1] / den) * outs[1]
                        + (es[2] / den) * outs[2])
    od = jnp.concatenate(od_slabs, axis=1)
    pa = jnp.dot(oa_ref[...], wa_ref[...], preferred_element_type=_F32)
    pb = jnp.dot(od.astype(_BF16), wb_ref[...], preferred_element_type=_F32)
    merged = (jax.nn.sigmoid(ga_ref[...].astype(_F32)) * pa
              + jax.nn.sigmoid(gb_ref[...].astype(_F32)) * pb)
    mo = jnp.dot(merged.astype(_BF16), wo_ref[...], preferred_element_type=_F32)
    out_ref[...] = x_ref[...] + _rms(mo, g_ref[...])


def _merge(x, oa, dil_outs, dil_lses, ga, gb, w_proj_a, w_proj_b, w_out, g_post):
    n, d = x.shape
    tm = min(TOK_TM, n)
    row = lambda width: pl.BlockSpec((tm, width), lambda i: (i, 0))
    slabs = pl.BlockSpec((DIL_GROUP_W // LANES, tm, LANES), lambda i: (0, i, 0))
    return pl.pallas_call(
        _merge_body,
        grid=(n // tm,),
        in_specs=[row(d), row(DA_V_W)] + [slabs] * 6 + [row(d), row(d),
                  _resident(w_proj_a.shape), _resident(w_proj_b.shape), _resident(w_out.shape),
                  _resident((1, d))],
        out_specs=row(d),
        out_shape=jax.ShapeDtypeStruct((n, d), _F32),
        compiler_params=_params(1),
        name="merge",
    )(x, oa, *dil_outs, *dil_lses, ga, gb, w_proj_a.astype(_BF16), w_proj_b.astype(_BF16),
      w_out.astype(_BF16), g_post.reshape(1, d))


def _rope_tables(positions):
    half = ROT_DIM // 2
    inv = ROPE_THETA ** (-(jnp.arange(0, ROT_DIM, 2, dtype=_F32) / ROT_DIM))
    ang = positions.astype(_F32).reshape(-1, 1) * inv
    cs = jnp.concatenate([jnp.cos(ang), jnp.sin(ang)], axis=1)
    terms = []
    for _ in range(3):
        head = lax.bitcast_convert_type(
            lax.bitcast_convert_type(cs, jnp.uint32) & jnp.uint32(0xFFFF0000), _F32)
        terms.append(head.astype(_BF16))
        cs = cs - head
    spread = np.zeros((ROT_DIM, 3 * LANES), np.float32)
    for lane in range(LANES):
        p = lane % DA_HEAD_DIM
        if p < ROT_DIM:
            spread[p % half, lane] = 1.0
        if half <= p < ROT_DIM:
            spread[p, LANES + lane] = 1.0
        if p < half:
            spread[half + p, 2 * LANES + lane] = -1.0
    return (jnp.concatenate(terms, axis=1),
            jnp.asarray(np.concatenate([spread] * 3, axis=0), dtype=_BF16))


def kernel(x, positions, w_in, lambda_q1, lambda_k1, lambda_q2, lambda_k2, g_subln, w_proj_a, w_proj_b, w_out, w_gu1, w_down1, w_gu2, w_down2, g_pre_ffn1, g_post_ffn1, g_pre_mix, g_post_mix, g_pre_ffn2, g_post_ffn2):
    b, s, d = x.shape
    n = b * s
    depth = w_in.shape[0]
    rope = _rope_tables(positions)
    xf = x.reshape(n, d)
    for l in range(depth):
        lambda_init = 0.8 - 0.6 * math.exp(-0.3 * l)
        xf = _ffn(xf, g_pre_ffn1[l], w_gu1[l], w_down1[l], g_post_ffn1[l])

        q, k, vt, qd, kd, vd, ga, gb = _inproj(xf, g_pre_mix[l], w_in[l], rope, b, s)
        lam_vecs = jnp.stack([lambda_q1[l], lambda_k1[l], lambda_q2[l], lambda_k2[l]], axis=0)
        oa = _diffattn(q.reshape(b, s, DA_QK_W), k.reshape(b, s, DA_QK_W), vt, lam_vecs,
                       g_subln[l], lambda_init)

        dil_outs, dil_lses = [], []
        for gi, (win, dil) in enumerate(DIL_PAIRS):
            o_g, lse_g = _dilattn(qd, kd, vd, gi, dil, win // (2 * dil), s)
            dil_outs.append(o_g)
            dil_lses.append(lse_g)

        xf = _merge(xf, oa.reshape(n, DA_V_W), dil_outs, dil_lses, ga, gb,
                    w_proj_a[l], w_proj_b[l], w_out[l], g_post_mix[l])
        xf = _ffn(xf, g_pre_ffn2[l], w_gu2[l], w_down2[l], g_post_ffn2[l])
    return xf.reshape(b, s, d)
```

```python
import functools
import math

import jax
import jax.numpy as jnp
import numpy as np
from jax import lax
from jax.experimental import pallas as pl
from jax.experimental.pallas import tpu as pltpu

D_MODEL = 1024
DA_HEADS = 8
DA_HEAD_DIM = 64
DA_V_DIM = 2 * DA_HEAD_DIM
DA_QK_W = DA_HEADS * 2 * DA_HEAD_DIM
DA_V_W = DA_HEADS * DA_V_DIM
DIL_PAIRS = ((128, 1), (512, 4), (2048, 16))
DIL_HEADS_PER_GROUP = 4
DIL_HEAD_DIM = 64
DIL_GROUP_W = DIL_HEADS_PER_GROUP * DIL_HEAD_DIM
DIL_W = DIL_GROUP_W * len(DIL_PAIRS)
ROPE_THETA = 500000.0
ROT_DIM = 16
D_FF = 2816
EPS = 1e-6
NEG = -1e30

LANES = 128
BF16_ROWS = 16
VMEM_LIMIT = 56 * 1024 * 1024

FFN_TM = 512
FFN_CHUNKS = 2
TOK_TM = 512
ATT_TQ = 512
ATT_TK = 1024
KEY_CHUNK = 256
DIL_T = 2048
DIL_MAX_STRIDE = 4
DIL_UNROLL = 16

_BF16 = jnp.bfloat16
_F32 = jnp.float32


def _params(n_axes):
    return pltpu.CompilerParams(dimension_semantics=("arbitrary",) * n_axes,
                                vmem_limit_bytes=VMEM_LIMIT)


def _resident(shape):
    zeros = (0,) * len(shape)
    return pl.BlockSpec(shape, lambda *_: zeros, pipeline_mode=pl.Buffered(1))


def _rms(x, g):
    ms = jnp.mean(x * x, axis=-1, keepdims=True)
    return x * lax.rsqrt(ms + EPS) * g


def _ffn_body(x_ref, gpre_ref, wgu_ref, wd_ref, gpost_ref, o_ref):
    rows = x_ref.shape[0] // FFN_CHUNKS
    for i in range(FFN_CHUNKS):
        x = x_ref[i * rows:(i + 1) * rows, :]
        h = _rms(x, gpre_ref[...]).astype(_BF16)
        gu = jnp.dot(h, wgu_ref[...], preferred_element_type=_F32)
        g = gu[:, :D_FF]
        u = gu[:, D_FF:]
        a = (g * jax.nn.sigmoid(g) * u).astype(_BF16)
        y = jnp.dot(a, wd_ref[...], preferred_element_type=_F32)
        o_ref[i * rows:(i + 1) * rows, :] = x + 0.5 * _rms(y, gpost_ref[...])


def _ffn(x, g_pre, w_gu, w_down, g_post):
    n, d = x.shape
    tm = min(FFN_TM, n)
    row = pl.BlockSpec((tm, d), lambda i: (i, 0))
    return pl.pallas_call(
        _ffn_body,
        grid=(n // tm,),
        in_specs=[row, _resident((1, d)), _resident(w_gu.shape), _resident(w_down.shape),
                  _resident((1, d))],
        out_specs=row,
        out_shape=jax.ShapeDtypeStruct((n, d), _F32),
        compiler_params=_params(1),
        name="ffn",
    )(x, g_pre.reshape(1, d), w_gu.astype(_BF16), w_down.astype(_BF16), g_post.reshape(1, d))


def _inproj_body(x_ref, g_ref, cs_ref, spread_ref, w_ref, wvt_ref,
                 q_ref, k_ref, vt_ref, qd_ref, kd_ref, vd_ref, ga_ref, gb_ref, *, q_scale, qd_scale):
    h = _rms(x_ref[...], g_ref[...]).astype(_BF16)
    pats = jnp.dot(cs_ref[...], spread_ref[...], preferred_element_type=_F32)
    lane = lax.broadcasted_iota(jnp.int32, (1, LANES), 1)
    cos = pats[:, :LANES] + (lane % DA_HEAD_DIM >= ROT_DIM).astype(_F32)
    sa = pats[:, LANES:2 * LANES]
    sb = pats[:, 2 * LANES:]

    def proj(start, width):
        return jnp.dot(h, w_ref[:, start:start + width], preferred_element_type=_F32)

    def rope(zj):
        return (zj * cos + pltpu.roll(zj, ROT_DIM // 2, 1) * sa
                + pltpu.roll(zj, LANES - ROT_DIM // 2, 1) * sb)

    def rope_store(z, out_ref, scale):
        for j in range(z.shape[1] // LANES):
            r = rope(z[:, j * LANES:(j + 1) * LANES]) * scale
            out_ref[:, j * LANES:(j + 1) * LANES] = r.astype(out_ref.dtype)

    def slab_store(z, out_ref, fn):
        for j in range(z.shape[1] // LANES):
            out_ref[j] = fn(z[:, j * LANES:(j + 1) * LANES])

    c = 0
    rope_store(proj(c, DA_QK_W), q_ref, q_scale); c += DA_QK_W
    rope_store(proj(c, DA_QK_W), k_ref, 1.0); c += DA_QK_W
    vt_ref[...] = lax.dot_general(wvt_ref[...], h, (((1,), (1,)), ((), ())),
                                  preferred_element_type=_F32).astype(vt_ref.dtype)
    c += DA_V_W
    slab_store(proj(c, DIL_W), qd_ref, lambda zj: rope(zj) * qd_scale); c += DIL_W
    slab_store(proj(c, DIL_W), kd_ref, rope); c += DIL_W
    slab_store(proj(c, DIL_W), vd_ref, lambda zj: zj); c += DIL_W
    ga_ref[...] = proj(c, D_MODEL).astype(ga_ref.dtype); c += D_MODEL
    gb_ref[...] = proj(c, D_MODEL).astype(gb_ref.dtype)


def _transpose_cast_body(x_ref, o_ref):
    o_ref[...] = x_ref[...].T.astype(o_ref.dtype)


def _transpose_cast(x):
    rows, cols = x.shape
    tc = min(2 * LANES, cols)
    return pl.pallas_call(
        _transpose_cast_body,
        grid=(cols // tc,),
        in_specs=[pl.BlockSpec((rows, tc), lambda i: (0, i))],
        out_specs=pl.BlockSpec((tc, rows), lambda i: (i, 0)),
        out_shape=jax.ShapeDtypeStruct((cols, rows), _BF16),
        compiler_params=_params(1),
        name="wv_transpose",
    )(x)


def _inproj(x, g_pre, w_in, rope, batch, seq):
    n, d = x.shape
    tm = min(TOK_TM, seq)
    ns = seq // tm
    tk = min(ATT_TK, seq)
    per_key_tile = tk // tm
    cs, spread = rope
    w = w_in.astype(_BF16)
    v0 = 2 * DA_QK_W
    wvt = _transpose_cast(w_in[:, v0:v0 + DA_V_W])
    row = lambda width: pl.BlockSpec((tm, width), lambda i: (i, 0))
    nslab = DIL_W // LANES
    slabs = pl.BlockSpec((nslab, tm, LANES), lambda i: (0, i, 0))
    body = functools.partial(
        _inproj_body,
        q_scale=math.log2(math.e) / math.sqrt(DA_HEAD_DIM),
        qd_scale=1.0 / math.sqrt(DIL_HEAD_DIM))
    return pl.pallas_call(
        body,
        grid=(n // tm,),
        in_specs=[row(d), _resident((1, d)), row(cs.shape[1]), _resident(spread.shape),
                  _resident(w.shape), _resident(wvt.shape)],
        out_specs=[row(DA_QK_W), row(DA_QK_W),
                   pl.BlockSpec((None, None, DA_V_W, tm),
                                lambda i: (i // ns, (i % ns) // per_key_tile, 0, i % per_key_tile)),
                   slabs, slabs, slabs, row(d), row(d)],
        out_shape=[jax.ShapeDtypeStruct((n, DA_QK_W), _BF16),
                   jax.ShapeDtypeStruct((n, DA_QK_W), _BF16),
                   jax.ShapeDtypeStruct((batch, seq // tk, DA_V_W, tk), _BF16),
                   jax.ShapeDtypeStruct((nslab, n, LANES), _F32),
                   jax.ShapeDtypeStruct((nslab, n, LANES), _F32),
                   jax.ShapeDtypeStruct((nslab, n, LANES), _F32),
                   jax.ShapeDtypeStruct((n, d), _BF16),
                   jax.ShapeDtypeStruct((n, d), _BF16)],
        compiler_params=_params(1),
        name="inproj",
    )(x, g_pre.reshape(1, d), cs, spread, w, wvt)


def _diffattn_body(lam_ref, gsub_ref, q_ref, k_ref, vt_ref, o_ref,
                   s_ref, mx_ref, m_ref, acc_ref, fin_ref, *, tq, lambda_init):
    nk, _, tk = vt_ref.shape
    total = (q_ref.shape[0] // tq) * nk
    lane = lax.broadcasted_iota(jnp.int32, (tq, DA_V_DIM), 1)
    ones_rows = (lax.broadcasted_iota(jnp.int32, (BF16_ROWS, tk), 0) == 0).astype(_BF16)
    fin_ref[...] = jnp.ones(fin_ref.shape, _F32)

    def masked_q(t):
        q = q_ref[pl.ds(pl.multiple_of((t // nk) * tq, tq), tq), :]
        zero = jnp.zeros_like(q)
        return [jnp.where((lane >= c * DA_HEAD_DIM) & (lane < (c + 1) * DA_HEAD_DIM), q, zero)
                for c in range(2)]

    def score_chunk(qc, t, slot, c, kc):
        rows = pl.ds(pl.multiple_of((t % nk) * tk + kc * KEY_CHUNK, KEY_CHUNK), KEY_CHUNK)
        s = lax.dot_general(k_ref[rows, :], qc[c], (((1,), (1,)), ((), ())),
                            preferred_element_type=_F32)
        s_ref[slot, c, kc * KEY_CHUNK:(kc + 1) * KEY_CHUNK, :] = s
        return jnp.max(s, axis=0, keepdims=True)

    def scores(t, slot):
        qc = masked_q(t)
        for c in range(2):
            cm = [score_chunk(qc, t, slot, c, kc) for kc in range(tk // KEY_CHUNK)]
            mx_ref[slot, c] = functools.reduce(jnp.maximum, cm)

    def step(t, j, slot):
        t_next = jnp.minimum(t + 1, total - 1)
        qc = masked_q(t_next)
        vt = jnp.concatenate([vt_ref[j], ones_rows], axis=0)
        m_new, alpha = [], []
        for c in range(2):
            if j == 0:
                m_new.append(mx_ref[slot, c])
                alpha.append(None)
            else:
                m_old = m_ref[c]
                m_new.append(jnp.maximum(m_old, mx_ref[slot, c]))
                alpha.append(jnp.exp2(m_old - m_new[c]))
            m_ref[c] = m_new[c]
        cmax, pv = [None, None], [None, None]
        for kc in range(tk // KEY_CHUNK):
            keys = slice(kc * KEY_CHUNK, (kc + 1) * KEY_CHUNK)
            for c in range(2):
                cm = score_chunk(qc, t_next, 1 - slot, c, kc)
                cmax[c] = cm if cmax[c] is None else jnp.maximum(cmax[c], cm)
                p = jnp.exp2(s_ref[slot, c, keys, :] - m_new[c]).astype(_BF16)
                d = jnp.dot(vt[:, keys], p, preferred_element_type=_F32)
                pv[c] = d if pv[c] is None else pv[c] + d
        dst = fin_ref if j == nk - 1 else acc_ref
        for c in range(2):
            mx_ref[1 - slot, c] = cmax[c]
            dst[c] = pv[c] if j == 0 else alpha[c] * acc_ref[c] + pv[c]

    def finalize(qi):
        lq1, lk1, lq2, lk2 = (lam_ref[i:i + 1, :] for i in range(4))
        lam = (jnp.exp(jnp.sum(lq1 * lk1, axis=1, keepdims=True))
               - jnp.exp(jnp.sum(lq2 * lk2, axis=1, keepdims=True)) + lambda_init)
        num = [fin_ref[c, :DA_V_DIM, :] for c in range(2)]
        den = [fin_ref[c, DA_V_DIM:DA_V_DIM + 1, :] for c in range(2)]
        o = num[0] / den[0] - lam * (num[1] / den[1])
        ms = jnp.mean(o * o, axis=0, keepdims=True)
        y = o * lax.rsqrt(ms + EPS) * gsub_ref[...] * (1.0 - lambda_init)
        o_ref[pl.ds(pl.multiple_of(qi * tq, tq), tq), :] = y.T.astype(o_ref.dtype)

    scores(0, 0)

    def trip(qi, carry):
        finalize(jnp.maximum(qi - 1, 0))
        for j in range(nk):
            step(qi * nk + j, j, j % 2)
        return carry

    lax.fori_loop(0, total // nk, trip, 0)
    finalize(total // nk - 1)


def _diffattn(q, k, vt, lam_vecs, g_subln, lambda_init):
    b, s, _ = q.shape
    tq = min(ATT_TQ, s)
    nk, tk = vt.shape[1], vt.shape[3]
    assert nk % 2 == 0 and s % tq == 0 and tk % KEY_CHUNK == 0
    body = functools.partial(_diffattn_body, tq=tq, lambda_init=lambda_init)
    head = pl.BlockSpec((None, s, DA_V_DIM), lambda bi, h: (bi, 0, h))
    return pl.pallas_call(
        body,
        grid=(b, DA_HEADS),
        in_specs=[_resident(lam_vecs.shape), _resident((DA_V_DIM, 1)), head, head,
                  pl.BlockSpec((None, nk, DA_V_DIM, tk), lambda bi, h: (bi, 0, h, 0))],
        out_specs=head,
        out_shape=jax.ShapeDtypeStruct((b, s, DA_V_W), _BF16),
        scratch_shapes=[pltpu.VMEM((2, 2, tk, tq), _F32),
                        pltpu.VMEM((2, 2, 1, tq), _F32),
                        pltpu.VMEM((2, 1, tq), _F32),
                        pltpu.VMEM((2, DA_V_DIM + BF16_ROWS, tq), _F32),
                        pltpu.VMEM((2, DA_V_DIM + BF16_ROWS, tq), _F32)],
        compiler_params=_params(2),
        name="diffattn",
    )(lam_vecs, g_subln.reshape(DA_V_DIM, 1), q, k, vt)


def _dilattn_body(q_ref, kp_ref, km_ref, kn_ref, vp_ref, vm_ref, vn_ref, o_ref, lse_ref, *folded,
                  dil, half, seq_len, tiles_per_seq, fold):
    t = q_ref.shape[1]
    per_res = t // dil
    qb, win = 2 * half, 4 * half
    n_sub = per_res // qb
    nh = DIL_HEADS_PER_GROUP
    nslab = DIL_GROUP_W // LANES
    base = (pl.program_id(0) % tiles_per_seq) * per_res
    inner = dil // fold
    head_of_lane = lax.broadcasted_iota(jnp.int32, (qb, DIL_GROUP_W), 1) // DIL_HEAD_DIM
    r_iota = lax.broadcasted_iota(jnp.int32, (nh * qb, win), 0) % qb
    j = lax.broadcasted_iota(jnp.int32, (nh * qb, win), 1)
    band_bias = jnp.where((j >= r_iota) & (j <= r_iota + 2 * half), 0.0, NEG).astype(_F32)
    j_row = lax.broadcasted_iota(jnp.int32, (1, win), 1)

    def rows(start, count, stride=inner):
        return pl.ds(start, count, stride=stride) if stride > 1 else pl.ds(start, count)

    def gather(ref, start, count):
        return jnp.concatenate([ref[sl, rows(start, count), :] for sl in range(nslab)],
                               axis=1).astype(_BF16)

    def window(prev_ref, main_ref, next_ref, res, i, row0):
        if n_sub == 1:
            lo, hi = gather(prev_ref, res, half), gather(next_ref, res, half)
        else:
            lo_main = gather(main_ref, jnp.maximum(row0 - half * dil, res), half)
            lo = jnp.where(i == 0, gather(prev_ref, res, half), lo_main)
            hi_main = gather(main_ref,
                             jnp.minimum(row0 + qb * dil, (per_res - half) * dil + res), half)
            hi = jnp.where(i == n_sub - 1, gather(next_ref, res, half), hi_main)
        return jnp.concatenate([lo, gather(main_ref, row0, qb), hi], axis=0)

    if fold > 1:
        qf_ref, kf_ref, vf_ref, of_ref, lf_ref = folded
        tf, hf = t // fold, half * inner
        for a in range(fold):
            for sl in range(nslab):
                qf_ref[a, sl] = q_ref[sl, rows(a, tf, fold), :]
                for dst, (p_ref, m_ref, n_ref) in ((kf_ref, (kp_ref, km_ref, kn_ref)),
                                                   (vf_ref, (vp_ref, vm_ref, vn_ref))):
                    dst[a, sl, 0:hf] = p_ref[sl, rows(a, hf, fold), :]
                    dst[a, sl, hf:hf + tf] = m_ref[sl, rows(a, tf, fold), :]
                    dst[a, sl, hf + tf:hf + tf + hf] = n_ref[sl, rows(a, hf, fold), :]

    def block(idx, carry):
        res, i = idx // n_sub, idx % n_sub
        if fold > 1:
            a = res % fold
            row0 = i * (qb * inner) + res // fold
            qi = gather(qf_ref.at[a], row0, qb)
            kw = gather(kf_ref.at[a], row0, win)
            vw = gather(vf_ref.at[a], row0, win)
            o_dst, lse_dst = of_ref.at[a], lf_ref.at[a]
        else:
            row0 = i * (qb * dil) + res
            qi = gather(q_ref, row0, qb)
            kw = window(kp_ref, km_ref, kn_ref, res, i, row0)
            vw = window(vp_ref, vm_ref, vn_ref, res, i, row0)
            o_dst, lse_dst = o_ref, lse_ref
        kpos = base + (i * qb - half) + j_row
        in_seq = (kpos >= 0) & (kpos < seq_len)
        zero = jnp.zeros_like(qi)
        qs = jnp.concatenate([jnp.where(head_of_lane == h, qi, zero) for h in range(nh)], axis=0)
        s = lax.dot_general(qs, kw, (((1,), (1,)), ((), ())), preferred_element_type=_F32)
        s = jnp.where(in_seq, s + band_bias, NEG)
        m = jnp.max(s, axis=1, keepdims=True)
        e = jnp.exp(s - m)
        l = jnp.sum(e, axis=1, keepdims=True)
        pv = jnp.dot(e.astype(_BF16), vw, preferred_element_type=_F32) / l
        lse = m + jnp.log(l)
        o = jnp.zeros((qb, DIL_GROUP_W), _F32)
        lse_full = jnp.zeros((qb, DIL_GROUP_W), _F32)
        for h in range(nh):
            sel = head_of_lane == h
            o = jnp.where(sel, pv[h * qb:(h + 1) * qb], o)
            lse_full = jnp.where(sel, lse[h * qb:(h + 1) * qb], lse_full)
        for sl in range(nslab):
            o_dst[sl, rows(row0, qb), :] = o[:, sl * LANES:(sl + 1) * LANES]
            lse_dst[sl, rows(row0, qb), :] = lse_full[:, sl * LANES:(sl + 1) * LANES]
        return carry

    lax.fori_loop(0, dil * n_sub, block, 0, unroll=DIL_UNROLL)

    if fold > 1:
        for a in range(fold):
            for sl in range(nslab):
                o_ref[sl, rows(a, tf, fold), :] = of_ref[a, sl]
                lse_ref[sl, rows(a, tf, fold), :] = lf_ref[a, sl]


def _dilattn(qd, kd, vd, group, dil, half, seq):
    nslab_all, n, _ = qd.shape
    nslab = DIL_GROUP_W // LANES
    t = min(DIL_T, seq)
    halo = half * dil
    assert seq % t == 0 and t % (2 * half * dil) == 0 and t % halo == 0 and half % 8 == 0
    tiles_per_seq = seq // t
    per_tile = t // halo
    last = n // halo - 1
    main = pl.BlockSpec((nslab, t, LANES), lambda i: (group, i, 0))
    prev = pl.BlockSpec((nslab, halo, LANES),
                        lambda i: (group, jnp.maximum(i * per_tile - 1, 0), 0))
    nxt = pl.BlockSpec((nslab, halo, LANES),
                       lambda i: (group, jnp.minimum((i + 1) * per_tile, last), 0))
    out = pl.BlockSpec((nslab, t, LANES), lambda i: (0, i, 0))
    fold = max(dil // DIL_MAX_STRIDE, 1)
    assert dil % fold == 0 and halo % fold == 0
    body = functools.partial(_dilattn_body, dil=dil, half=half, seq_len=seq // dil,
                             tiles_per_seq=tiles_per_seq, fold=fold)
    folded = []
    if fold > 1:
        tile = pltpu.VMEM((fold, nslab, t // fold, LANES), _F32)
        with_halos = pltpu.VMEM((fold, nslab, (t + 2 * halo) // fold, LANES), _F32)
        folded = [tile, with_halos, with_halos, tile, tile]
    return pl.pallas_call(
        body,
        grid=(n // t,),
        in_specs=[main, prev, main, nxt, prev, main, nxt],
        out_specs=[out, out],
        out_shape=[jax.ShapeDtypeStruct((nslab, n, LANES), _F32)] * 2,
        scratch_shapes=folded,
        compiler_params=_params(1),
        name=f"dilattn_d{dil}",
    )(qd, kd, kd, kd, vd, vd, vd)


def _merge_body(x_ref, oa_ref, o0_ref, o1_ref, o2_ref, l0_ref, l1_ref, l2_ref, ga_ref, gb_ref,
                wa_ref, wb_ref, wo_ref, g_ref, out_ref):
    od_slabs = []
    for sl in range(DIL_GROUP_W // LANES):
        lses = (l0_ref[sl], l1_ref[sl], l2_ref[sl])
        outs = (o0_ref[sl], o1_ref[sl], o2_ref[sl])
        m = jnp.maximum(jnp.maximum(lses[0], lses[1]), lses[2])
        es = [jnp.exp(l - m) for l in lses]
        den = es[0] + es[1] + es[2]
        od_slabs.append((es[0] / den) * outs[0] + (es[1] / den) * outs[1]
                        + (es[2] / den) * outs[2])
    od = jnp.concatenate(od_slabs, axis=1)
    pa = jnp.dot(oa_ref[...], wa_ref[...], preferred_element_type=_F32)
    pb = jnp.dot(od.astype(_BF16), wb_ref[...], preferred_element_type=_F32)
    merged = (jax.nn.sigmoid(ga_ref[...].astype(_F32)) * pa
              + jax.nn.sigmoid(gb_ref[...].astype(_F32)) * pb)
    mo = jnp.dot(merged.astype(_BF16), wo_ref[...], preferred_element_type=_F32)
    out_ref[...] = x_ref[...] + _rms(mo, g_ref[...])


def _merge(x, oa, dil_outs, dil_lses, ga, gb, w_proj_a, w_proj_b, w_out, g_post):
    n, d = x.shape
    tm = min(TOK_TM, n)
    row = lambda width: pl.BlockSpec((tm, width), lambda i: (i, 0))
    slabs = pl.BlockSpec((DIL_GROUP_W // LANES, tm, LANES), lambda i: (0, i, 0))
    return pl.pallas_call(
        _merge_body,
        grid=(n // tm,),
        in_specs=[row(d), row(DA_V_W)] + [slabs] * 6 + [row(d), row(d),
                  _resident(w_proj_a.shape), _resident(w_proj_b.shape), _resident(w_out.shape),
                  _resident((1, d))],
        out_specs=row(d),
        out_shape=jax.ShapeDtypeStruct((n, d), _F32),
        compiler_params=_params(1),
        name="merge",
    )(x, oa, *dil_outs, *dil_lses, ga, gb, w_proj_a.astype(_BF16), w_proj_b.astype(_BF16),
      w_out.astype(_BF16), g_post.reshape(1, d))


def _rope_tables(positions):
    half = ROT_DIM // 2
    inv = ROPE_THETA ** (-(jnp.arange(0, ROT_DIM, 2, dtype=_F32) / ROT_DIM))
    ang = positions.astype(_F32).reshape(-1, 1) * inv
    cs = jnp.concatenate([jnp.cos(ang), jnp.sin(ang)], axis=1)
    terms = []
    for _ in range(3):
        head = lax.bitcast_convert_type(
            lax.bitcast_convert_type(cs, jnp.uint32) & jnp.uint32(0xFFFF0000), _F32)
        terms.append(head.astype(_BF16))
        cs = cs - head
    spread = np.zeros((ROT_DIM, 3 * LANES), np.float32)
    for lane in range(LANES):
        p = lane % DA_HEAD_DIM
        if p < ROT_DIM:
            spread[p % half, lane] = 1.0
        if half <= p < ROT_DIM:
            spread[p, LANES + lane] = 1.0
        if p < half:
            spread[half + p, 2 * LANES + lane] = -1.0
    return (jnp.concatenate(terms, axis=1),
            jnp.asarray(np.concatenate([spread] * 3, axis=0), dtype=_BF16))


def kernel(x, positions, w_in, lambda_q1, lambda_k1, lambda_q2, lambda_k2, g_subln, w_proj_a, w_proj_b, w_out, w_gu1, w_down1, w_gu2, w_down2, g_pre_ffn1, g_post_ffn1, g_pre_mix, g_post_mix, g_pre_ffn2, g_post_ffn2):
    b, s, d = x.shape
    n = b * s
    depth = w_in.shape[0]
    rope = _rope_tables(positions)
    xf = x.reshape(n, d)
    for l in range(depth):
        lambda_init = 0.8 - 0.6 * math.exp(-0.3 * l)
        xf = _ffn(xf, g_pre_ffn1[l], w_gu1[l], w_down1[l], g_post_ffn1[l])

        q, k, vt, qd, kd, vd, ga, gb = _inproj(xf, g_pre_mix[l], w_in[l], rope, b, s)
        lam_vecs = jnp.stack([lambda_q1[l], lambda_k1[l], lambda_q2[l], lambda_k2[l]], axis=0)
        oa = _diffattn(q.reshape(b, s, DA_QK_W), k.reshape(b, s, DA_QK_W), vt, lam_vecs,
                       g_subln[l], lambda_init)

        dil_outs, dil_lses = [], []
        for gi, (win, dil) in enumerate(DIL_PAIRS):
            o_g, lse_g = _dilattn(qd, kd, vd, gi, dil, win // (2 * dil), s)
            dil_outs.append(o_g)
            dil_lses.append(lse_g)

        xf = _merge(xf, oa.reshape(n, DA_V_W), dil_outs, dil_lses, ga, gb,
                    w_proj_a[l], w_proj_b[l], w_out[l], g_post_mix[l])
        xf = _ffn(xf, g_pre_ffn2[l], w_gu2[l], w_down2[l], g_post_ffn2[l])
    return xf.reshape(b, s, d)
```

```python
import functools
import math

import jax
import jax.numpy as jnp
import numpy as np
from jax import lax
from jax.experimental import pallas as pl
from jax.experimental.pallas import tpu as pltpu

D_MODEL = 1024
DA_HEADS = 8
DA_HEAD_DIM = 64
DA_V_DIM = 2 * DA_HEAD_DIM
DA_QK_W = DA_HEADS * 2 * DA_HEAD_DIM
DA_V_W = DA_HEADS * DA_V_DIM
DIL_PAIRS = ((128, 1), (512, 4), (2048, 16))
DIL_HEADS_PER_GROUP = 4
DIL_HEAD_DIM = 64
DIL_GROUP_W = DIL_HEADS_PER_GROUP * DIL_HEAD_DIM
DIL_W = DIL_GROUP_W * len(DIL_PAIRS)
ROPE_THETA = 500000.0
ROT_DIM = 16
D_FF = 2816
EPS = 1e-6
NEG = -1e30

LANES = 128
BF16_ROWS = 16
VMEM_LIMIT = 56 * 1024 * 1024

FFN_TM = 512
FFN_CHUNKS = 2
TOK_TM = 512
ATT_TQ = 256
ATT_TK = 2048
KEY_CHUNK = 256
DIL_T = 2048
DIL_MAX_STRIDE = 4
DIL_UNROLL = 16

_BF16 = jnp.bfloat16
_F32 = jnp.float32


def _params(n_axes):
    return pltpu.CompilerParams(dimension_semantics=("arbitrary",) * n_axes,
                                vmem_limit_bytes=VMEM_LIMIT)


def _resident(shape):
    zeros = (0,) * len(shape)
    return pl.BlockSpec(shape, lambda *_: zeros, pipeline_mode=pl.Buffered(1))


def _rms(x, g):
    ms = jnp.mean(x * x, axis=-1, keepdims=True)
    return x * lax.rsqrt(ms + EPS) * g


def _ffn_body(x_ref, gpre_ref, wgu_ref, wd_ref, gpost_ref, o_ref):
    rows = x_ref.shape[0] // FFN_CHUNKS
    for i in range(FFN_CHUNKS):
        x = x_ref[i * rows:(i + 1) * rows, :]
        h = _rms(x, gpre_ref[...]).astype(_BF16)
        gu = jnp.dot(h, wgu_ref[...], preferred_element_type=_F32)
        g = gu[:, :D_FF]
        u = gu[:, D_FF:]
        a = (g * jax.nn.sigmoid(g) * u).astype(_BF16)
        y = jnp.dot(a, wd_ref[...], preferred_element_type=_F32)
        o_ref[i * rows:(i + 1) * rows, :] = x + 0.5 * _rms(y, gpost_ref[...])


def _ffn(x, g_pre, w_gu, w_down, g_post):
    n, d = x.shape
    tm = min(FFN_TM, n)
    row = pl.BlockSpec((tm, d), lambda i: (i, 0))
    return pl.pallas_call(
        _ffn_body,
        grid=(n // tm,),
        in_specs=[row, _resident((1, d)), _resident(w_gu.shape), _resident(w_down.shape),
                  _resident((1, d))],
        out_specs=row,
        out_shape=jax.ShapeDtypeStruct((n, d), _F32),
        compiler_params=_params(1),
        name="ffn",
    )(x, g_pre.reshape(1, d), w_gu.astype(_BF16), w_down.astype(_BF16), g_post.reshape(1, d))


def _inproj_body(x_ref, g_ref, cs_ref, spread_ref, w_ref, wvt_ref,
                 q_ref, k_ref, vt_ref, qd_ref, kd_ref, vd_ref, ga_ref, gb_ref, *, q_scale, qd_scale):
    h = _rms(x_ref[...], g_ref[...]).astype(_BF16)
    pats = jnp.dot(cs_ref[...], spread_ref[...], preferred_element_type=_F32)
    lane = lax.broadcasted_iota(jnp.int32, (1, LANES), 1)
    cos = pats[:, :LANES] + (lane % DA_HEAD_DIM >= ROT_DIM).astype(_F32)
    sa = pats[:, LANES:2 * LANES]
    sb = pats[:, 2 * LANES:]

    def proj(start, width):
        return jnp.dot(h, w_ref[:, start:start + width], preferred_element_type=_F32)

    def rope(zj):
        return (zj * cos + pltpu.roll(zj, ROT_DIM // 2, 1) * sa
                + pltpu.roll(zj, LANES - ROT_DIM // 2, 1) * sb)

    def rope_store(z, out_ref, scale):
        for j in range(z.shape[1] // LANES):
            r = rope(z[:, j * LANES:(j + 1) * LANES]) * scale
            out_ref[:, j * LANES:(j + 1) * LANES] = r.astype(out_ref.dtype)

    def slab_store(z, out_ref, fn):
        for j in range(z.shape[1] // LANES):
            out_ref[j] = fn(z[:, j * LANES:(j + 1) * LANES])

    c = 0
    rope_store(proj(c, DA_QK_W), q_ref, q_scale); c += DA_QK_W
    rope_store(proj(c, DA_QK_W), k_ref, 1.0); c += DA_QK_W
    vt_ref[...] = lax.dot_general(wvt_ref[...], h, (((1,), (1,)), ((), ())),
                                  preferred_element_type=_F32).astype(vt_ref.dtype)
    c += DA_V_W
    slab_store(proj(c, DIL_W), qd_ref, lambda zj: rope(zj) * qd_scale); c += DIL_W
    slab_store(proj(c, DIL_W), kd_ref, rope); c += DIL_W
    slab_store(proj(c, DIL_W), vd_ref, lambda zj: zj); c += DIL_W
    ga_ref[...] = proj(c, D_MODEL).astype(ga_ref.dtype); c += D_MODEL
    gb_ref[...] = proj(c, D_MODEL).astype(gb_ref.dtype)


def _transpose_cast_body(x_ref, o_ref):
    o_ref[...] = x_ref[...].T.astype(o_ref.dtype)


def _transpose_cast(x):
    rows, cols = x.shape
    tc = min(2 * LANES, cols)
    return pl.pallas_call(
        _transpose_cast_body,
        grid=(cols // tc,),
        in_specs=[pl.BlockSpec((rows, tc), lambda i: (0, i))],
        out_specs=pl.BlockSpec((tc, rows), lambda i: (i, 0)),
        out_shape=jax.ShapeDtypeStruct((cols, rows), _BF16),
        compiler_params=_params(1),
        name="wv_transpose",
    )(x)


def _inproj(x, g_pre, w_in, rope, batch, seq):
    n, d = x.shape
    tm = min(TOK_TM, seq)
    ns = seq // tm
    tk = min(ATT_TK, seq)
    per_key_tile = tk // tm
    cs, spread = rope
    w = w_in.astype(_BF16)
    v0 = 2 * DA_QK_W
    wvt = _transpose_cast(w_in[:, v0:v0 + DA_V_W])
    row = lambda width: pl.BlockSpec((tm, width), lambda i: (i, 0))
    nslab = DIL_W // LANES
    slabs = pl.BlockSpec((nslab, tm, LANES), lambda i: (0, i, 0))
    body = functools.partial(
        _inproj_body,
        q_scale=math.log2(math.e) / math.sqrt(DA_HEAD_DIM),
        qd_scale=1.0 / math.sqrt(DIL_HEAD_DIM))
    return pl.pallas_call(
        body,
        grid=(n // tm,),
        in_specs=[row(d), _resident((1, d)), row(cs.shape[1]), _resident(spread.shape),
                  _resident(w.shape), _resident(wvt.shape)],
        out_specs=[row(DA_QK_W), row(DA_QK_W),
                   pl.BlockSpec((None, None, DA_V_W, tm),
                                lambda i: (i // ns, (i % ns) // per_key_tile, 0, i % per_key_tile)),
                   slabs, slabs, slabs, row(d), row(d)],
        out_shape=[jax.ShapeDtypeStruct((n, DA_QK_W), _BF16),
                   jax.ShapeDtypeStruct((n, DA_QK_W), _BF16),
                   jax.ShapeDtypeStruct((batch, seq // tk, DA_V_W, tk), _BF16),
                   jax.ShapeDtypeStruct((nslab, n, LANES), _F32),
                   jax.ShapeDtypeStruct((nslab, n, LANES), _F32),
                   jax.ShapeDtypeStruct((nslab, n, LANES), _F32),
                   jax.ShapeDtypeStruct((n, d), _BF16),
                   jax.ShapeDtypeStruct((n, d), _BF16)],
        compiler_params=_params(1),
        name="inproj",
    )(x, g_pre.reshape(1, d), cs, spread, w, wvt)


def _diffattn_body(lam_ref, gsub_ref, q_ref, k_ref, vt_ref, o_ref,
                   s_ref, mx_ref, m_ref, acc_ref, fin_ref, *, tq, lambda_init):
    nk, _, tk = vt_ref.shape
    total = (q_ref.shape[0] // tq) * nk
    lane = lax.broadcasted_iota(jnp.int32, (tq, DA_V_DIM), 1)
    ones_rows = (lax.broadcasted_iota(jnp.int32, (BF16_ROWS, tk), 0) == 0).astype(_BF16)
    fin_ref[...] = jnp.ones(fin_ref.shape, _F32)

    def masked_q(t):
        q = q_ref[pl.ds(pl.multiple_of((t // nk) * tq, tq), tq), :]
        zero = jnp.zeros_like(q)
        return [jnp.where((lane >= c * DA_HEAD_DIM) & (lane < (c + 1) * DA_HEAD_DIM), q, zero)
                for c in range(2)]

    def score_chunk(qc, t, slot, c, kc):
        rows = pl.ds(pl.multiple_of((t % nk) * tk + kc * KEY_CHUNK, KEY_CHUNK), KEY_CHUNK)
        s = lax.dot_general(k_ref[rows, :], qc[c], (((1,), (1,)), ((), ())),
                            preferred_element_type=_F32)
        s_ref[slot, c, kc * KEY_CHUNK:(kc + 1) * KEY_CHUNK, :] = s
        return jnp.max(s, axis=0, keepdims=True)

    def scores(t, slot):
        qc = masked_q(t)
        for c in range(2):
            cm = [score_chunk(qc, t, slot, c, kc) for kc in range(tk // KEY_CHUNK)]
            mx_ref[slot, c] = functools.reduce(jnp.maximum, cm)

    def step(t, j, slot):
        t_next = jnp.minimum(t + 1, total - 1)
        qc = masked_q(t_next)
        vt = jnp.concatenate([vt_ref[j], ones_rows], axis=0)
        m_new, alpha = [], []
        for c in range(2):
            if j == 0:
                m_new.append(mx_ref[slot, c])
                alpha.append(None)
            else:
                m_old = m_ref[c]
                m_new.append(jnp.maximum(m_old, mx_ref[slot, c]))
                alpha.append(jnp.exp2(m_old - m_new[c]))
            m_ref[c] = m_new[c]
        cmax, pv = [None, None], [None, None]
        for kc in range(tk // KEY_CHUNK):
            keys = slice(kc * KEY_CHUNK, (kc + 1) * KEY_CHUNK)
            for c in range(2):
                cm = score_chunk(qc, t_next, 1 - slot, c, kc)
                cmax[c] = cm if cmax[c] is None else jnp.maximum(cmax[c], cm)
                p = jnp.exp2(s_ref[slot, c, keys, :] - m_new[c]).astype(_BF16)
                d = jnp.dot(vt[:, keys], p, preferred_element_type=_F32)
                pv[c] = d if pv[c] is None else pv[c] + d
        dst = fin_ref if j == nk - 1 else acc_ref
        for c in range(2):
            mx_ref[1 - slot, c] = cmax[c]
            dst[c] = pv[c] if j == 0 else alpha[c] * acc_ref[c] + pv[c]

    def finalize(qi):
        lq1, lk1, lq2, lk2 = (lam_ref[i:i + 1, :] for i in range(4))
        lam = (jnp.exp(jnp.sum(lq1 * lk1, axis=1, keepdims=True))
               - jnp.exp(jnp.sum(lq2 * lk2, axis=1, keepdims=True)) + lambda_init)
        num = [fin_ref[c, :DA_V_DIM, :] for c in range(2)]
        den = [fin_ref[c, DA_V_DIM:DA_V_DIM + 1, :] for c in range(2)]
        o = num[0] / den[0] - lam * (num[1] / den[1])
        ms = jnp.mean(o * o, axis=0, keepdims=True)
        y = o * lax.rsqrt(ms + EPS) * gsub_ref[...] * (1.0 - lambda_init)
        o_ref[pl.ds(pl.multiple_of(qi * tq, tq), tq), :] = y.T.astype(o_ref.dtype)

    scores(0, 0)

    def trip(qi, carry):
        finalize(jnp.maximum(qi - 1, 0))
        for j in range(nk):
            step(qi * nk + j, j, j % 2)
        return carry

    lax.fori_loop(0, total // nk, trip, 0)
    finalize(total // nk - 1)


def _diffattn(q, k, vt, lam_vecs, g_subln, lambda_init):
    b, s, _ = q.shape
    tq = min(ATT_TQ, s)
    nk, tk = vt.shape[1], vt.shape[3]
    assert nk % 2 == 0 and s % tq == 0 and tk % KEY_CHUNK == 0
    body = functools.partial(_diffattn_body, tq=tq, lambda_init=lambda_init)
    head = pl.BlockSpec((None, s, DA_V_DIM), lambda bi, h: (bi, 0, h))
    return pl.pallas_call(
        body,
        grid=(b, DA_HEADS),
        in_specs=[_resident(lam_vecs.shape), _resident((DA_V_DIM, 1)), head, head,
                  pl.BlockSpec((None, nk, DA_V_DIM, tk), lambda bi, h: (bi, 0, h, 0))],
        out_specs=head,
        out_shape=jax.ShapeDtypeStruct((b, s, DA_V_W), _BF16),
        scratch_shapes=[pltpu.VMEM((2, 2, tk, tq), _F32),
                        pltpu.VMEM((2, 2, 1, tq), _F32),
                        pltpu.VMEM((2, 1, tq), _F32),
                        pltpu.VMEM((2, DA_V_DIM + BF16_ROWS, tq), _F32),
                        pltpu.VMEM((2, DA_V_DIM + BF16_ROWS, tq), _F32)],
        compiler_params=_params(2),
        name="diffattn",
    )(lam_vecs, g_subln.reshape(DA_V_DIM, 1), q, k, vt)


def _dilattn_body(q_ref, kp_ref, km_ref, kn_ref, vp_ref, vm_ref, vn_ref, o_ref, lse_ref, *folded,
                  dil, half, seq_len, tiles_per_seq, fold):
    t = q_ref.shape[1]
    per_res = t // dil
    qb, win = 2 * half, 4 * half
    n_sub = per_res // qb
    nh = DIL_HEADS_PER_GROUP
    nslab = DIL_GROUP_W // LANES
    base = (pl.program_id(0) % tiles_per_seq) * per_res
    inner = dil // fold
    head_of_lane = lax.broadcasted_iota(jnp.int32, (qb, DIL_GROUP_W), 1) // DIL_HEAD_DIM
    r_iota = lax.broadcasted_iota(jnp.int32, (nh * qb, win), 0) % qb
    j = lax.broadcasted_iota(jnp.int32, (nh * qb, win), 1)
    band_bias = jnp.where((j >= r_iota) & (j <= r_iota + 2 * half), 0.0, NEG).astype(_F32)
    j_row = lax.broadcasted_iota(jnp.int32, (1, win), 1)

    def rows(start, count, stride=inner):
        return pl.ds(start, count, stride=stride) if stride > 1 else pl.ds(start, count)

    def gather(ref, start, count):
        return jnp.concatenate([ref[sl, rows(start, count), :] for sl in range(nslab)],
                               axis=1).astype(_BF16)

    def window(prev_ref, main_ref, next_ref, res, i, row0):
        if n_sub == 1:
            lo, hi = gather(prev_ref, res, half), gather(next_ref, res, half)
        else:
            lo_main = gather(main_ref, jnp.maximum(row0 - half * dil, res), half)
            lo = jnp.where(i == 0, gather(prev_ref, res, half), lo_main)
            hi_main = gather(main_ref,
                             jnp.minimum(row0 + qb * dil, (per_res - half) * dil + res), half)
            hi = jnp.where(i == n_sub - 1, gather(next_ref, res, half), hi_main)
        return jnp.concatenate([lo, gather(main_ref, row0, qb), hi], axis=0)

    if fold > 1:
        qf_ref, kf_ref, vf_ref, of_ref, lf_ref = folded
        tf, hf = t // fold, half * inner
        for a in range(fold):
            for sl in range(nslab):
                qf_ref[a, sl] = q_ref[sl, rows(a, tf, fold), :]
                for dst, (p_ref, m_ref, n_ref) in ((kf_ref, (kp_ref, km_ref, kn_ref)),
                                                   (vf_ref, (vp_ref, vm_ref, vn_ref))):
                    dst[a, sl, 0:hf] = p_ref[sl, rows(a, hf, fold), :]
                    dst[a, sl, hf:hf + tf] = m_ref[sl, rows(a, tf, fold), :]
                    dst[a, sl, hf + tf:hf + tf + hf] = n_ref[sl, rows(a, hf, fold), :]

    def block(idx, carry):
        res, i = idx // n_sub, idx % n_sub
        if fold > 1:
            a = res % fold
            row0 = i * (qb * inner) + res // fold
            qi = gather(qf_ref.at[a], row0, qb)
            kw = gather(kf_ref.at[a], row0, win)
            vw = gather(vf_ref.at[a], row0, win)
            o_dst, lse_dst = of_ref.at[a], lf_ref.at[a]
        else:
            row0 = i * (qb * dil) + res
            qi = gather(q_ref, row0, qb)
            kw = window(kp_ref, km_ref, kn_ref, res, i, row0)
            vw = window(vp_ref, vm_ref, vn_ref, res, i, row0)
            o_dst, lse_dst = o_ref, lse_ref
        kpos = base + (i * qb - half) + j_row
        in_seq = (kpos >= 0) & (kpos < seq_len)
        zero = jnp.zeros_like(qi)
        qs = jnp.concatenate([jnp.where(head_of_lane == h, qi, zero) for h in range(nh)], axis=0)
        s = lax.dot_general(qs, kw, (((1,), (1,)), ((), ())), preferred_element_type=_F32)
        s = jnp.where(in_seq, s + band_bias, NEG)
        m = jnp.max(s, axis=1, keepdims=True)
        e = jnp.exp(s - m)
        l = jnp.sum(e, axis=1, keepdims=True)
        pv = jnp.dot(e.astype(_BF16), vw, preferred_element_type=_F32) / l
        lse = m + jnp.log(l)
        o = jnp.zeros((qb, DIL_GROUP_W), _F32)
        lse_full = jnp.zeros((qb, DIL_GROUP_W), _F32)
        for h in range(nh):
            sel = head_of_lane == h
            o = jnp.where(sel, pv[h * qb:(h + 1) * qb], o)
            lse_full = jnp.where(sel, lse[h * qb:(h + 1) * qb], lse_full)
        for sl in range(nslab):
            o_dst[sl, rows(row0, qb), :] = o[:, sl * LANES:(sl + 1) * LANES]
            lse_dst[sl, rows(row0, qb), :] = lse_full[:, sl * LANES:(sl + 1) * LANES]
        return carry

    lax.fori_loop(0, dil * n_sub, block, 0, unroll=DIL_UNROLL)

    if fold > 1:
        for a in range(fold):
            for sl in range(nslab):
                o_ref[sl, rows(a, tf, fold), :] = of_ref[a, sl]
                lse_ref[sl, rows(a, tf, fold), :] = lf_ref[a, sl]


def _dilattn(qd, kd, vd, group, dil, half, seq):
    nslab_all, n, _ = qd.shape
    nslab = DIL_GROUP_W // LANES
    t = min(DIL_T, seq)
    halo = half * dil
    assert seq % t == 0 and t % (2 * half * dil) == 0 and t % halo == 0 and half % 8 == 0
    tiles_per_seq = seq // t
    per_tile = t // halo
    last = n // halo - 1
    main = pl.BlockSpec((nslab, t, LANES), lambda i: (group, i, 0))
    prev = pl.BlockSpec((nslab, halo, LANES),
                        lambda i: (group, jnp.maximum(i * per_tile - 1, 0), 0))
    nxt = pl.BlockSpec((nslab, halo, LANES),
                       lambda i: (group, jnp.minimum((i + 1) * per_tile, last), 0))
    out = pl.BlockSpec((nslab, t, LANES), lambda i: (0, i, 0))
    fold = max(dil // DIL_MAX_STRIDE, 1)
    assert dil % fold == 0 and halo % fold == 0
    body = functools.partial(_dilattn_body, dil=dil, half=half, seq_len=seq // dil,
                             tiles_per_seq=tiles_per_seq, fold=fold)
    folded = []
    if fold > 1:
        tile = pltpu.VMEM((fold, nslab, t // fold, LANES), _F32)
        with_halos = pltpu.VMEM((fold, nslab, (t + 2 * halo) // fold, LANES), _F32)
        folded = [tile, with_halos, with_halos, tile, tile]
    return pl.pallas_call(
        body,
        grid=(n // t,),
        in_specs=[main, prev, main, nxt, prev, main, nxt],
        out_specs=[out, out],
        out_shape=[jax.ShapeDtypeStruct((nslab, n, LANES), _F32)] * 2,
        scratch_shapes=folded,
        compiler_params=_params(1),
        name=f"dilattn_d{dil}",
    )(qd, kd, kd, kd, vd, vd, vd)


def _merge_body(x_ref, oa_ref, o0_ref, o1_ref, o2_ref, l0_ref, l1_ref, l2_ref, ga_ref, gb_ref,
                wa_ref, wb_ref, wo_ref, g_ref, out_ref):
    od_slabs = []
    for sl in range(DIL_GROUP_W // LANES):
        lses = (l0_ref[sl], l1_ref[sl], l2_ref[sl])
        outs = (o0_ref[sl], o1_ref[sl], o2_ref[sl])
        m = jnp.maximum(jnp.maximum(lses[0], lses[1]), lses[2])
        es = [jnp.exp(l - m) for l in lses]
        den = es[0] + es[1] + es[2]
        od_slabs.append((es[0] / den) * outs[0] + (es[1] / den) * outs[1]
                        + (es[2] / den) * outs[2])
    od = jnp.concatenate(od_slabs, axis=1)
    pa = jnp.dot(oa_ref[...], wa_ref[...], preferred_element_type=_F32)
    pb = jnp.dot(od.astype(_BF16), wb_ref[...], preferred_element_type=_F32)
    merged = (jax.nn.sigmoid(ga_ref[...].astype(_F32)) * pa
              + jax.nn.sigmoid(gb_ref[...].astype(_F32)) * pb)
    mo = jnp.dot(merged.astype(_BF16), wo_ref[...], preferred_element_type=_F32)
    out_ref[...] = x_ref[...] + _rms(mo, g_ref[...])


def _merge(x, oa, dil_outs, dil_lses, ga, gb, w_proj_a, w_proj_b, w_out, g_post):
    n, d = x.shape
    tm = min(TOK_TM, n)
    row = lambda width: pl.BlockSpec((tm, width), lambda i: (i, 0))
    slabs = pl.BlockSpec((DIL_GROUP_W // LANES, tm, LANES), lambda i: (0, i, 0))
    return pl.pallas_call(
        _merge_body,
        grid=(n // tm,),
        in_specs=[row(d), row(DA_V_W)] + [slabs] * 6 + [row(d), row(d),
                  _resident(w_proj_a.shape), _resident(w_proj_b.shape), _resident(w_out.shape),
                  _resident((1, d))],
        out_specs=row(d),
        out_shape=jax.ShapeDtypeStruct((n, d), _F32),
        compiler_params=_params(1),
        name="merge",
    )(x, oa, *dil_outs, *dil_lses, ga, gb, w_proj_a.astype(_BF16), w_proj_b.astype(_BF16),
      w_out.astype(_BF16), g_post.reshape(1, d))


def _rope_tables(positions):
    half = ROT_DIM // 2
    inv = ROPE_THETA ** (-(jnp.arange(0, ROT_DIM, 2, dtype=_F32) / ROT_DIM))
    ang = positions.astype(_F32).reshape(-1, 1) * inv
    cs = jnp.concatenate([jnp.cos(ang), jnp.sin(ang)], axis=1)
    terms = []
    for _ in range(3):
        head = lax.bitcast_convert_type(
            lax.bitcast_convert_type(cs, jnp.uint32) & jnp.uint32(0xFFFF0000), _F32)
        terms.append(head.astype(_BF16))
        cs = cs - head
    spread = np.zeros((ROT_DIM, 3 * LANES), np.float32)
    for lane in range(LANES):
        p = lane % DA_HEAD_DIM
        if p < ROT_DIM:
            spread[p % half, lane] = 1.0
        if half <= p < ROT_DIM:
            spread[p, LANES + lane] = 1.0
        if p < half:
            spread[half + p, 2 * LANES + lane] = -1.0
    return (jnp.concatenate(terms, axis=1),
            jnp.asarray(np.concatenate([spread] * 3, axis=0), dtype=_BF16))


def kernel(x, positions, w_in, lambda_q1, lambda_k1, lambda_q2, lambda_k2, g_subln, w_proj_a, w_proj_b, w_out, w_gu1, w_down1, w_gu2, w_down2, g_pre_ffn1, g_post_ffn1, g_pre_mix, g_post_mix, g_pre_ffn2, g_post_ffn2):
    b, s, d = x.shape
    n = b * s
    depth = w_in.shape[0]
    rope = _rope_tables(positions)
    xf = x.reshape(n, d)
    for l in range(depth):
        lambda_init = 0.8 - 0.6 * math.exp(-0.3 * l)
        xf = _ffn(xf, g_pre_ffn1[l], w_gu1[l], w_down1[l], g_post_ffn1[l])

        q, k, vt, qd, kd, vd, ga, gb = _inproj(xf, g_pre_mix[l], w_in[l], rope, b, s)
        lam_vecs = jnp.stack([lambda_q1[l], lambda_k1[l], lambda_q2[l], lambda_k2[l]], axis=0)
        oa = _diffattn(q.reshape(b, s, DA_QK_W), k.reshape(b, s, DA_QK_W), vt, lam_vecs,
                       g_subln[l], lambda_init)

        dil_outs, dil_lses = [], []
        for gi, (win, dil) in enumerate(DIL_PAIRS):
            o_g, lse_g = _dilattn(qd, kd, vd, gi, dil, win // (2 * dil), s)
            dil_outs.append(o_g)
            dil_lses.append(lse_g)

        xf = _merge(xf, oa.reshape(n, DA_V_W), dil_outs, dil_lses, ga, gb,
                    w_proj_a[l], w_proj_b[l], w_out[l], g_post_mix[l])
        xf = _ffn(xf, g_pre_ffn2[l], w_gu2[l], w_down2[l], g_post_ffn2[l])
    return xf.reshape(b, s, d)
```

```python
import functools
import math

import jax
import jax.numpy as jnp
import numpy as np
from jax import lax
from jax.experimental import pallas as pl
from jax.experimental.pallas import tpu as pltpu

D_MODEL = 1024
DA_HEADS = 8
DA_HEAD_DIM = 64
DA_V_DIM = 2 * DA_HEAD_DIM
DA_QK_W = DA_HEADS * 2 * DA_HEAD_DIM
DA_V_W = DA_HEADS * DA_V_DIM
DIL_PAIRS = ((128, 1), (512, 4), (2048, 16))
DIL_HEADS_PER_GROUP = 4
DIL_HEAD_DIM = 64
DIL_GROUP_W = DIL_HEADS_PER_GROUP * DIL_HEAD_DIM
DIL_W = DIL_GROUP_W * len(DIL_PAIRS)
ROPE_THETA = 500000.0
ROT_DIM = 16
D_FF = 2816
EPS = 1e-6
NEG = -1e30

LANES = 128
BF16_ROWS = 16
VMEM_LIMIT = 56 * 1024 * 1024

FFN_TM = 512
FFN_CHUNKS = 2
TOK_TM = 512
ATT_TQ = 512
ATT_TK = 2048
KEY_CHUNK = 256
DIL_T = 2048
DIL_MAX_STRIDE = 4
DIL_UNROLL = 16

_BF16 = jnp.bfloat16
_F32 = jnp.float32


def _params(n_axes):
    return pltpu.CompilerParams(dimension_semantics=("arbitrary",) * n_axes,
                                vmem_limit_bytes=VMEM_LIMIT)


def _resident(shape):
    zeros = (0,) * len(shape)
    return pl.BlockSpec(shape, lambda *_: zeros, pipeline_mode=pl.Buffered(1))


def _rms(x, g):
    ms = jnp.mean(x * x, axis=-1, keepdims=True)
    return x * lax.rsqrt(ms + EPS) * g


def _ffn_body(x_ref, gpre_ref, wgu_ref, wd_ref, gpost_ref, o_ref):
    rows = x_ref.shape[0] // FFN_CHUNKS
    for i in range(FFN_CHUNKS):
        x = x_ref[i * rows:(i + 1) * rows, :]
        h = _rms(x, gpre_ref[...]).astype(_BF16)
        gu = jnp.dot(h, wgu_ref[...], preferred_element_type=_F32)
        g = gu[:, :D_FF]
        u = gu[:, D_FF:]
        a = (g * jax.nn.sigmoid(g) * u).astype(_BF16)
        y = jnp.dot(a, wd_ref[...], preferred_element_type=_F32)
        o_ref[i * rows:(i + 1) * rows, :] = x + 0.5 * _rms(y, gpost_ref[...])


def _ffn(x, g_pre, w_gu, w_down, g_post):
    n, d = x.shape
    tm = min(FFN_TM, n)
    row = pl.BlockSpec((tm, d), lambda i: (i, 0))
    return pl.pallas_call(
        _ffn_body,
        grid=(n // tm,),
        in_specs=[row, _resident((1, d)), _resident(w_gu.shape), _resident(w_down.shape),
                  _resident((1, d))],
        out_specs=row,
        out_shape=jax.ShapeDtypeStruct((n, d), _F32),
        compiler_params=_params(1),
        name="ffn",
    )(x, g_pre.reshape(1, d), w_gu.astype(_BF16), w_down.astype(_BF16), g_post.reshape(1, d))


def _inproj_body(x_ref, g_ref, cs_ref, spread_ref, w_ref, wvt_ref,
                 q_ref, k_ref, vt_ref, qd_ref, kd_ref, vd_ref, ga_ref, gb_ref, *, q_scale, qd_scale):
    h = _rms(x_ref[...], g_ref[...]).astype(_BF16)
    pats = jnp.dot(cs_ref[...], spread_ref[...], preferred_element_type=_F32)
    lane = lax.broadcasted_iota(jnp.int32, (1, LANES), 1)
    cos = pats[:, :LANES] + (lane % DA_HEAD_DIM >= ROT_DIM).astype(_F32)
    sa = pats[:, LANES:2 * LANES]
    sb = pats[:, 2 * LANES:]

    def proj(start, width):
        return jnp.dot(h, w_ref[:, start:start + width], preferred_element_type=_F32)

    def rope(zj):
        return (zj * cos + pltpu.roll(zj, ROT_DIM // 2, 1) * sa
                + pltpu.roll(zj, LANES - ROT_DIM // 2, 1) * sb)

    def rope_store(z, out_ref, scale):
        for j in range(z.shape[1] // LANES):
            r = rope(z[:, j * LANES:(j + 1) * LANES]) * scale
            out_ref[:, j * LANES:(j + 1) * LANES] = r.astype(out_ref.dtype)

    def slab_store(z, out_ref, fn):
        for j in range(z.shape[1] // LANES):
            out_ref[j] = fn(z[:, j * LANES:(j + 1) * LANES])

    c = 0
    rope_store(proj(c, DA_QK_W), q_ref, q_scale); c += DA_QK_W
    rope_store(proj(c, DA_QK_W), k_ref, 1.0); c += DA_QK_W
    vt_ref[...] = lax.dot_general(wvt_ref[...], h, (((1,), (1,)), ((), ())),
                                  preferred_element_type=_F32).astype(vt_ref.dtype)
    c += DA_V_W
    slab_store(proj(c, DIL_W), qd_ref, lambda zj: rope(zj) * qd_scale); c += DIL_W
    slab_store(proj(c, DIL_W), kd_ref, rope); c += DIL_W
    slab_store(proj(c, DIL_W), vd_ref, lambda zj: zj); c += DIL_W
    ga_ref[...] = proj(c, D_MODEL).astype(ga_ref.dtype); c += D_MODEL
    gb_ref[...] = proj(c, D_MODEL).astype(gb_ref.dtype)


def _transpose_cast_body(x_ref, o_ref):
    o_ref[...] = x_ref[...].T.astype(o_ref.dtype)


def _transpose_cast(x):
    rows, cols = x.shape
    tc = min(2 * LANES, cols)
    return pl.pallas_call(
        _transpose_cast_body,
        grid=(cols // tc,),
        in_specs=[pl.BlockSpec((rows, tc), lambda i: (0, i))],
        out_specs=pl.BlockSpec((tc, rows), lambda i: (i, 0)),
        out_shape=jax.ShapeDtypeStruct((cols, rows), _BF16),
        compiler_params=_params(1),
        name="wv_transpose",
    )(x)


def _inproj(x, g_pre, w_in, rope, batch, seq):
    n, d = x.shape
    tm = min(TOK_TM, seq)
    ns = seq // tm
    tk = min(ATT_TK, seq)
    per_key_tile = tk // tm
    cs, spread = rope
    w = w_in.astype(_BF16)
    v0 = 2 * DA_QK_W
    wvt = _transpose_cast(w_in[:, v0:v0 + DA_V_W])
    row = lambda width: pl.BlockSpec((tm, width), lambda i: (i, 0))
    nslab = DIL_W // LANES
    slabs = pl.BlockSpec((nslab, tm, LANES), lambda i: (0, i, 0))
    body = functools.partial(
        _inproj_body,
        q_scale=math.log2(math.e) / math.sqrt(DA_HEAD_DIM),
        qd_scale=1.0 / math.sqrt(DIL_HEAD_DIM))
    return pl.pallas_call(
        body,
        grid=(n // tm,),
        in_specs=[row(d), _resident((1, d)), row(cs.shape[1]), _resident(spread.shape),
                  _resident(w.shape), _resident(wvt.shape)],
        out_specs=[row(DA_QK_W), row(DA_QK_W),
                   pl.BlockSpec((None, None, DA_V_W, tm),
                                lambda i: (i // ns, (i % ns) // per_key_tile, 0, i % per_key_tile)),
                   slabs, slabs, slabs, row(d), row(d)],
        out_shape=[jax.ShapeDtypeStruct((n, DA_QK_W), _BF16),
                   jax.ShapeDtypeStruct((n, DA_QK_W), _BF16),
                   jax.ShapeDtypeStruct((batch, seq // tk, DA_V_W, tk), _BF16),
                   jax.ShapeDtypeStruct((nslab, n, LANES), _F32),
                   jax.ShapeDtypeStruct((nslab, n, LANES), _F32),
                   jax.ShapeDtypeStruct((nslab, n, LANES), _F32),
                   jax.ShapeDtypeStruct((n, d), _BF16),
                   jax.ShapeDtypeStruct((n, d), _BF16)],
        compiler_params=_params(1),
        name="inproj",
    )(x, g_pre.reshape(1, d), cs, spread, w, wvt)


def _diffattn_body(lam_ref, gsub_ref, q_ref, k_ref, vt_ref, o_ref,
                   s_ref, mx_ref, m_ref, acc_ref, fin_ref, *, tq, lambda_init):
    nk, _, tk = vt_ref.shape
    total = (q_ref.shape[0] // tq) * nk
    lane = lax.broadcasted_iota(jnp.int32, (tq, DA_V_DIM), 1)
    ones_rows = (lax.broadcasted_iota(jnp.int32, (BF16_ROWS, tk), 0) == 0).astype(_BF16)
    fin_ref[...] = jnp.ones(fin_ref.shape, _F32)

    def masked_q(t):
        q = q_ref[pl.ds(pl.multiple_of((t // nk) * tq, tq), tq), :]
        zero = jnp.zeros_like(q)
        return [jnp.where((lane >= c * DA_HEAD_DIM) & (lane < (c + 1) * DA_HEAD_DIM), q, zero)
                for c in range(2)]

    def score_chunk(qc, t, slot, c, kc):
        rows = pl.ds(pl.multiple_of((t % nk) * tk + kc * KEY_CHUNK, KEY_CHUNK), KEY_CHUNK)
        s = lax.dot_general(k_ref[rows, :], qc[c], (((1,), (1,)), ((), ())),
                            preferred_element_type=_F32)
        s_ref[slot, c, kc * KEY_CHUNK:(kc + 1) * KEY_CHUNK, :tq] = s
        return jnp.max(s, axis=0, keepdims=True)

    def scores(t, slot):
        qc = masked_q(t)
        for c in range(2):
            cm = [score_chunk(qc, t, slot, c, kc) for kc in range(tk // KEY_CHUNK)]
            mx_ref[slot, c] = functools.reduce(jnp.maximum, cm)

    def step(t, j, slot):
        t_next = jnp.minimum(t + 1, total - 1)
        qc = masked_q(t_next)
        vt = jnp.concatenate([vt_ref[j], ones_rows], axis=0)
        m_new, alpha = [], []
        for c in range(2):
            if j == 0:
                m_new.append(mx_ref[slot, c])
                alpha.append(None)
            else:
                m_old = m_ref[c]
                m_new.append(jnp.maximum(m_old, mx_ref[slot, c]))
                alpha.append(jnp.exp2(m_old - m_new[c]))
            m_ref[c] = m_new[c]
        cmax, pv = [None, None], [None, None]
        for kc in range(tk // KEY_CHUNK):
            keys = slice(kc * KEY_CHUNK, (kc + 1) * KEY_CHUNK)
            for c in range(2):
                cm = score_chunk(qc, t_next, 1 - slot, c, kc)
                cmax[c] = cm if cmax[c] is None else jnp.maximum(cmax[c], cm)
                p = jnp.exp2(s_ref[slot, c, keys, :tq] - m_new[c]).astype(_BF16)
                d = jnp.dot(vt[:, keys], p, preferred_element_type=_F32)
                pv[c] = d if pv[c] is None else pv[c] + d
        dst = fin_ref if j == nk - 1 else acc_ref
        for c in range(2):
            mx_ref[1 - slot, c] = cmax[c]
            dst[c] = pv[c] if j == 0 else alpha[c] * acc_ref[c] + pv[c]

    def finalize(qi):
        lq1, lk1, lq2, lk2 = (lam_ref[i:i + 1, :] for i in range(4))
        lam = (jnp.exp(jnp.sum(lq1 * lk1, axis=1, keepdims=True))
               - jnp.exp(jnp.sum(lq2 * lk2, axis=1, keepdims=True)) + lambda_init)
        num = [fin_ref[c, :DA_V_DIM, :] for c in range(2)]
        den = [fin_ref[c, DA_V_DIM:DA_V_DIM + 1, :] for c in range(2)]
        o = num[0] / den[0] - lam * (num[1] / den[1])
        ms = jnp.mean(o * o, axis=0, keepdims=True)
        y = o * lax.rsqrt(ms + EPS) * gsub_ref[...] * (1.0 - lambda_init)
        o_ref[pl.ds(pl.multiple_of(qi * tq, tq), tq), :] = y.T.astype(o_ref.dtype)

    scores(0, 0)

    def trip(qi, carry):
        finalize(jnp.maximum(qi - 1, 0))
        for j in range(nk):
            step(qi * nk + j, j, j % 2)
        return carry

    lax.fori_loop(0, total // nk, trip, 0)
    finalize(total // nk - 1)


def _diffattn(q, k, vt, lam_vecs, g_subln, lambda_init):
    b, s, _ = q.shape
    tq = min(ATT_TQ, s)
    nk, tk = vt.shape[1], vt.shape[3]
    assert nk % 2 == 0 and s % tq == 0 and tk % KEY_CHUNK == 0
    body = functools.partial(_diffattn_body, tq=tq, lambda_init=lambda_init)
    head = pl.BlockSpec((None, s, DA_V_DIM), lambda bi, h: (bi, 0, h))
    return pl.pallas_call(
        body,
        grid=(b, DA_HEADS),
        in_specs=[_resident(lam_vecs.shape), _resident((DA_V_DIM, 1)), head, head,
                  pl.BlockSpec((None, nk, DA_V_DIM, tk), lambda bi, h: (bi, 0, h, 0))],
        out_specs=head,
        out_shape=jax.ShapeDtypeStruct((b, s, DA_V_W), _BF16),
        scratch_shapes=[pltpu.VMEM((2, 2, tk, tq + LANES), _F32),
                        pltpu.VMEM((2, 2, 1, tq), _F32),
                        pltpu.VMEM((2, 1, tq), _F32),
                        pltpu.VMEM((2, DA_V_DIM + BF16_ROWS, tq), _F32),
                        pltpu.VMEM((2, DA_V_DIM + BF16_ROWS, tq), _F32)],
        compiler_params=_params(2),
        name="diffattn",
    )(lam_vecs, g_subln.reshape(DA_V_DIM, 1), q, k, vt)


def _dilattn_body(q_ref, kp_ref, km_ref, kn_ref, vp_ref, vm_ref, vn_ref, o_ref, lse_ref, *folded,
                  dil, half, seq_len, tiles_per_seq, fold):
    t = q_ref.shape[1]
    per_res = t // dil
    qb, win = 2 * half, 4 * half
    n_sub = per_res // qb
    nh = DIL_HEADS_PER_GROUP
    nslab = DIL_GROUP_W // LANES
    base = (pl.program_id(0) % tiles_per_seq) * per_res
    inner = dil // fold
    head_of_lane = lax.broadcasted_iota(jnp.int32, (qb, DIL_GROUP_W), 1) // DIL_HEAD_DIM
    r_iota = lax.broadcasted_iota(jnp.int32, (nh * qb, win), 0) % qb
    j = lax.broadcasted_iota(jnp.int32, (nh * qb, win), 1)
    band_bias = jnp.where((j >= r_iota) & (j <= r_iota + 2 * half), 0.0, NEG).astype(_F32)
    j_row = lax.broadcasted_iota(jnp.int32, (1, win), 1)

    def rows(start, count, stride=inner):
        return pl.ds(start, count, stride=stride) if stride > 1 else pl.ds(start, count)

    def gather(ref, start, count):
        return jnp.concatenate([ref[sl, rows(start, count), :] for sl in range(nslab)],
                               axis=1).astype(_BF16)

    def window(prev_ref, main_ref, next_ref, res, i, row0):
        if n_sub == 1:
            lo, hi = gather(prev_ref, res, half), gather(next_ref, res, half)
        else:
            lo_main = gather(main_ref, jnp.maximum(row0 - half * dil, res), half)
            lo = jnp.where(i == 0, gather(prev_ref, res, half), lo_main)
            hi_main = gather(main_ref,
                             jnp.minimum(row0 + qb * dil, (per_res - half) * dil + res), half)
            hi = jnp.where(i == n_sub - 1, gather(next_ref, res, half), hi_main)
        return jnp.concatenate([lo, gather(main_ref, row0, qb), hi], axis=0)

    if fold > 1:
        qf_ref, kf_ref, vf_ref, of_ref, lf_ref = folded
        tf, hf = t // fold, half * inner
        for a in range(fold):
            for sl in range(nslab):
                qf_ref[a, sl] = q_ref[sl, rows(a, tf, fold), :]
                for dst, (p_ref, m_ref, n_ref) in ((kf_ref, (kp_ref, km_ref, kn_ref)),
                                                   (vf_ref, (vp_ref, vm_ref, vn_ref))):
                    dst[a, sl, 0:hf] = p_ref[sl, rows(a, hf, fold), :]
                    dst[a, sl, hf:hf + tf] = m_ref[sl, rows(a, tf, fold), :]
                    dst[a, sl, hf + tf:hf + tf + hf] = n_ref[sl, rows(a, hf, fold), :]

    def block(idx, carry):
        res, i = idx // n_sub, idx % n_sub
        if fold > 1:
            a = res % fold
            row0 = i * (qb * inner) + res // fold
            qi = gather(qf_ref.at[a], row0, qb)
            kw = gather(kf_ref.at[a], row0, win)
            vw = gather(vf_ref.at[a], row0, win)
            o_dst, lse_dst = of_ref.at[a], lf_ref.at[a]
        else:
            row0 = i * (qb * dil) + res
            qi = gather(q_ref, row0, qb)
            kw = window(kp_ref, km_ref, kn_ref, res, i, row0)
            vw = window(vp_ref, vm_ref, vn_ref, res, i, row0)
            o_dst, lse_dst = o_ref, lse_ref
        kpos = base + (i * qb - half) + j_row
        in_seq = (kpos >= 0) & (kpos < seq_len)
        zero = jnp.zeros_like(qi)
        qs = jnp.concatenate([jnp.where(head_of_lane == h, qi, zero) for h in range(nh)], axis=0)
        s = lax.dot_general(qs, kw, (((1,), (1,)), ((), ())), preferred_element_type=_F32)
        s = jnp.where(in_seq, s + band_bias, NEG)
        m = jnp.max(s, axis=1, keepdims=True)
        e = jnp.exp(s - m)
        l = jnp.sum(e, axis=1, keepdims=True)
        pv = jnp.dot(e.astype(_BF16), vw, preferred_element_type=_F32) / l
        lse = m + jnp.log(l)
        o = jnp.zeros((qb, DIL_GROUP_W), _F32)
        lse_full = jnp.zeros((qb, DIL_GROUP_W), _F32)
        for h in range(nh):
            sel = head_of_lane == h
            o = jnp.where(sel, pv[h * qb:(h + 1) * qb], o)
            lse_full = jnp.where(sel, lse[h * qb:(h + 1) * qb], lse_full)
        for sl in range(nslab):
            o_dst[sl, rows(row0, qb), :] = o[:, sl * LANES:(sl + 1) * LANES]
            lse_dst[sl, rows(row0, qb), :] = lse_full[:, sl * LANES:(sl + 1) * LANES]
        return carry

    lax.fori_loop(0, dil * n_sub, block, 0, unroll=DIL_UNROLL)

    if fold > 1:
        for a in range(fold):
            for sl in range(nslab):
                o_ref[sl, rows(a, tf, fold), :] = of_ref[a, sl]
                lse_ref[sl, rows(a, tf, fold), :] = lf_ref[a, sl]


def _dilattn(qd, kd, vd, group, dil, half, seq):
    nslab_all, n, _ = qd.shape
    nslab = DIL_GROUP_W // LANES
    t = min(DIL_T, seq)
    halo = half * dil
    assert seq % t == 0 and t % (2 * half * dil) == 0 and t % halo == 0 and half % 8 == 0
    tiles_per_seq = seq // t
    per_tile = t // halo
    last = n // halo - 1
    main = pl.BlockSpec((nslab, t, LANES), lambda i: (group, i, 0))
    prev = pl.BlockSpec((nslab, halo, LANES),
                        lambda i: (group, jnp.maximum(i * per_tile - 1, 0), 0))
    nxt = pl.BlockSpec((nslab, halo, LANES),
                       lambda i: (group, jnp.minimum((i + 1) * per_tile, last), 0))
    out = pl.BlockSpec((nslab, t, LANES), lambda i: (0, i, 0))
    fold = max(dil // DIL_MAX_STRIDE, 1)
    assert dil % fold == 0 and halo % fold == 0
    body = functools.partial(_dilattn_body, dil=dil, half=half, seq_len=seq // dil,
                             tiles_per_seq=tiles_per_seq, fold=fold)
    folded = []
    if fold > 1:
        tile = pltpu.VMEM((fold, nslab, t // fold, LANES), _F32)
        with_halos = pltpu.VMEM((fold, nslab, (t + 2 * halo) // fold, LANES), _F32)
        folded = [tile, with_halos, with_halos, tile, tile]
    return pl.pallas_call(
        body,
        grid=(n // t,),
        in_specs=[main, prev, main, nxt, prev, main, nxt],
        out_specs=[out, out],
        out_shape=[jax.ShapeDtypeStruct((nslab, n, LANES), _F32)] * 2,
        scratch_shapes=folded,
        compiler_params=_params(1),
        name=f"dilattn_d{dil}",
    )(qd, kd, kd, kd, vd, vd, vd)


def _merge_body(x_ref, oa_ref, o0_ref, o1_ref, o2_ref, l0_ref, l1_ref, l2_ref, ga_ref, gb_ref,
                wa_ref, wb_ref, wo_ref, g_ref, out_ref):
    od_slabs = []
    for sl in range(DIL_GROUP_W // LANES):
        lses = (l0_ref[sl], l1_ref[sl], l2_ref[sl])
        outs = (o0_ref[sl], o1_ref[sl], o2_ref[sl])
        m = jnp.maximum(jnp.maximum(lses[0], lses[1]), lses[2])
        es = [jnp.exp(l - m) for l in lses]
        den = es[0] + es[1] + es[2]
        od_slabs.append((es[0] / den) * outs[0] + (es[1] / den) * outs[1]
                        + (es[2] / den) * outs[2])
    od = jnp.concatenate(od_slabs, axis=1)
    pa = jnp.dot(oa_ref[...], wa_ref[...], preferred_element_type=_F32)
    pb = jnp.dot(od.astype(_BF16), wb_ref[...], preferred_element_type=_F32)
    merged = (jax.nn.sigmoid(ga_ref[...].astype(_F32)) * pa
              + jax.nn.sigmoid(gb_ref[...].astype(_F32)) * pb)
    mo = jnp.dot(merged.astype(_BF16), wo_ref[...], preferred_element_type=_F32)
    out_ref[...] = x_ref[...] + _rms(mo, g_ref[...])


def _merge(x, oa, dil_outs, dil_lses, ga, gb, w_proj_a, w_proj_b, w_out, g_post):
    n, d = x.shape
    tm = min(TOK_TM, n)
    row = lambda width: pl.BlockSpec((tm, width), lambda i: (i, 0))
    slabs = pl.BlockSpec((DIL_GROUP_W // LANES, tm, LANES), lambda i: (0, i, 0))
    return pl.pallas_call(
        _merge_body,
        grid=(n // tm,),
        in_specs=[row(d), row(DA_V_W)] + [slabs] * 6 + [row(d), row(d),
                  _resident(w_proj_a.shape), _resident(w_proj_b.shape), _resident(w_out.shape),
                  _resident((1, d))],
        out_specs=row(d),
        out_shape=jax.ShapeDtypeStruct((n, d), _F32),
        compiler_params=_params(1),
        name="merge",
    )(x, oa, *dil_outs, *dil_lses, ga, gb, w_proj_a.astype(_BF16), w_proj_b.astype(_BF16),
      w_out.astype(_BF16), g_post.reshape(1, d))


def _rope_tables(positions):
    half = ROT_DIM // 2
    inv = ROPE_THETA ** (-(jnp.arange(0, ROT_DIM, 2, dtype=_F32) / ROT_DIM))
    ang = positions.astype(_F32).reshape(-1, 1) * inv
    cs = jnp.concatenate([jnp.cos(ang), jnp.sin(ang)], axis=1)
    terms = []
    for _ in range(3):
        head = lax.bitcast_convert_type(
            lax.bitcast_convert_type(cs, jnp.uint32) & jnp.uint32(0xFFFF0000), _F32)
        terms.append(head.astype(_BF16))
        cs = cs - head
    spread = np.zeros((ROT_DIM, 3 * LANES), np.float32)
    for lane in range(LANES):
        p = lane % DA_HEAD_DIM
        if p < ROT_DIM:
            spread[p % half, lane] = 1.0
        if half <= p < ROT_DIM:
            spread[p, LANES + lane] = 1.0
        if p < half:
            spread[half + p, 2 * LANES + lane] = -1.0
    return (jnp.concatenate(terms, axis=1),
            jnp.asarray(np.concatenate([spread] * 3, axis=0), dtype=_BF16))


def kernel(x, positions, w_in, lambda_q1, lambda_k1, lambda_q2, lambda_k2, g_subln, w_proj_a, w_proj_b, w_out, w_gu1, w_down1, w_gu2, w_down2, g_pre_ffn1, g_post_ffn1, g_pre_mix, g_post_mix, g_pre_ffn2, g_post_ffn2):
    b, s, d = x.shape
    n = b * s
    depth = w_in.shape[0]
    rope = _rope_tables(positions)
    xf = x.reshape(n, d)
    for l in range(depth):
        lambda_init = 0.8 - 0.6 * math.exp(-0.3 * l)
        xf = _ffn(xf, g_pre_ffn1[l], w_gu1[l], w_down1[l], g_post_ffn1[l])

        q, k, vt, qd, kd, vd, ga, gb = _inproj(xf, g_pre_mix[l], w_in[l], rope, b, s)
        lam_vecs = jnp.stack([lambda_q1[l], lambda_k1[l], lambda_q2[l], lambda_k2[l]], axis=0)
        oa = _diffattn(q.reshape(b, s, DA_QK_W), k.reshape(b, s, DA_QK_W), vt, lam_vecs,
                       g_subln[l], lambda_init)

        dil_outs, dil_lses = [], []
        for gi, (win, dil) in enumerate(DIL_PAIRS):
            o_g, lse_g = _dilattn(qd, kd, vd, gi, dil, win // (2 * dil), s)
            dil_outs.append(o_g)
            dil_lses.append(lse_g)

        xf = _merge(xf, oa.reshape(n, DA_V_W), dil_outs, dil_lses, ga, gb,
                    w_proj_a[l], w_proj_b[l], w_out[l], g_post_mix[l])
        xf = _ffn(xf, g_pre_ffn2[l], w_gu2[l], w_down2[l], g_post_ffn2[l])
    return xf.reshape(b, s, d)
```

```python
import functools
import math

import jax
import jax.numpy as jnp
import numpy as np
from jax import lax
from jax.experimental import pallas as pl
from jax.experimental.pallas import tpu as pltpu

D_MODEL = 1024
DA_HEADS = 8
DA_HEAD_DIM = 64
DA_V_DIM = 2 * DA_HEAD_DIM
DA_QK_W = DA_HEADS * 2 * DA_HEAD_DIM
DA_V_W = DA_HEADS * DA_V_DIM
DIL_PAIRS = ((128, 1), (512, 4), (2048, 16))
DIL_HEADS_PER_GROUP = 4
DIL_HEAD_DIM = 64
DIL_GROUP_W = DIL_HEADS_PER_GROUP * DIL_HEAD_DIM
DIL_W = DIL_GROUP_W * len(DIL_PAIRS)
ROPE_THETA = 500000.0
ROT_DIM = 16
D_FF = 2816
EPS = 1e-6
NEG = -1e30

LANES = 128
BF16_ROWS = 16
VMEM_LIMIT = 56 * 1024 * 1024

FFN_TM = 512
FFN_CHUNKS = 2
TOK_TM = 512
ATT_TQ = 512
ATT_TK = 2048
KEY_CHUNK = 256
DIL_T = 2048
DIL_MAX_STRIDE = 4
DIL_UNROLL = 16

_BF16 = jnp.bfloat16
_F32 = jnp.float32


def _params(n_axes):
    return pltpu.CompilerParams(dimension_semantics=("arbitrary",) * n_axes,
                                vmem_limit_bytes=VMEM_LIMIT)


def _resident(shape):
    zeros = (0,) * len(shape)
    return pl.BlockSpec(shape, lambda *_: zeros, pipeline_mode=pl.Buffered(1))


def _rms(x, g):
    ms = jnp.mean(x * x, axis=-1, keepdims=True)
    return x * lax.rsqrt(ms + EPS) * g


def _ffn_body(x_ref, gpre_ref, wgu_ref, wd_ref, gpost_ref, o_ref):
    rows = x_ref.shape[0] // FFN_CHUNKS
    for i in range(FFN_CHUNKS):
        x = x_ref[i * rows:(i + 1) * rows, :]
        h = _rms(x, gpre_ref[...]).astype(_BF16)
        gu = jnp.dot(h, wgu_ref[...], preferred_element_type=_F32)
        g = gu[:, :D_FF]
        u = gu[:, D_FF:]
        a = (g * jax.nn.sigmoid(g) * u).astype(_BF16)
        y = jnp.dot(a, wd_ref[...], preferred_element_type=_F32)
        o_ref[i * rows:(i + 1) * rows, :] = x + 0.5 * _rms(y, gpost_ref[...])


def _ffn(x, g_pre, w_gu, w_down, g_post):
    n, d = x.shape
    tm = min(FFN_TM, n)
    row = pl.BlockSpec((tm, d), lambda i: (i, 0))
    return pl.pallas_call(
        _ffn_body,
        grid=(n // tm,),
        in_specs=[row, _resident((1, d)), _resident(w_gu.shape), _resident(w_down.shape),
                  _resident((1, d))],
        out_specs=row,
        out_shape=jax.ShapeDtypeStruct((n, d), _F32),
        compiler_params=_params(1),
        name="ffn",
    )(x, g_pre.reshape(1, d), w_gu.astype(_BF16), w_down.astype(_BF16), g_post.reshape(1, d))


def _inproj_body(x_ref, g_ref, cs_ref, spread_ref, w_ref, wvt_ref,
                 q_ref, k_ref, vt_ref, qd_ref, kd_ref, vd_ref, ga_ref, gb_ref, *, q_scale, qd_scale):
    h = _rms(x_ref[...], g_ref[...]).astype(_BF16)
    pats = jnp.dot(cs_ref[...], spread_ref[...], preferred_element_type=_F32)
    lane = lax.broadcasted_iota(jnp.int32, (1, LANES), 1)
    cos = pats[:, :LANES] + (lane % DA_HEAD_DIM >= ROT_DIM).astype(_F32)
    sa = pats[:, LANES:2 * LANES]
    sb = pats[:, 2 * LANES:]

    def proj(start, width):
        return jnp.dot(h, w_ref[:, start:start + width], preferred_element_type=_F32)

    def rope(zj):
        return (zj * cos + pltpu.roll(zj, ROT_DIM // 2, 1) * sa
                + pltpu.roll(zj, LANES - ROT_DIM // 2, 1) * sb)

    def rope_store(z, out_ref, scale):
        for j in range(z.shape[1] // LANES):
            r = rope(z[:, j * LANES:(j + 1) * LANES]) * scale
            out_ref[:, j * LANES:(j + 1) * LANES] = r.astype(out_ref.dtype)

    def slab_store(z, out_ref, fn):
        for j in range(z.shape[1] // LANES):
            out_ref[j] = fn(z[:, j * LANES:(j + 1) * LANES])

    c = 0
    rope_store(proj(c, DA_QK_W), q_ref, q_scale); c += DA_QK_W
    rope_store(proj(c, DA_QK_W), k_ref, 1.0); c += DA_QK_W
    vt_ref[...] = lax.dot_general(wvt_ref[...], h, (((1,), (1,)), ((), ())),
                                  preferred_element_type=_F32).astype(vt_ref.dtype)
    c += DA_V_W
    slab_store(proj(c, DIL_W), qd_ref, lambda zj: rope(zj) * qd_scale); c += DIL_W
    slab_store(proj(c, DIL_W), kd_ref, rope); c += DIL_W
    slab_store(proj(c, DIL_W), vd_ref, lambda zj: zj); c += DIL_W
    ga_ref[...] = proj(c, D_MODEL).astype(ga_ref.dtype); c += D_MODEL
    gb_ref[...] = proj(c, D_MODEL).astype(gb_ref.dtype)


def _transpose_cast_body(x_ref, o_ref):
    o_ref[...] = x_ref[...].T.astype(o_ref.dtype)


def _transpose_cast(x):
    rows, cols = x.shape
    tc = min(2 * LANES, cols)
    return pl.pallas_call(
        _transpose_cast_body,
        grid=(cols // tc,),
        in_specs=[pl.BlockSpec((rows, tc), lambda i: (0, i))],
        out_specs=pl.BlockSpec((tc, rows), lambda i: (i, 0)),
        out_shape=jax.ShapeDtypeStruct((cols, rows), _BF16),
        compiler_params=_params(1),
        name="wv_transpose",
    )(x)


def _inproj(x, g_pre, w_in, rope, batch, seq):
    n, d = x.shape
    tm = min(TOK_TM, seq)
    ns = seq // tm
    tk = min(ATT_TK, seq)
    per_key_tile = tk // tm
    cs, spread = rope
    w = w_in.astype(_BF16)
    v0 = 2 * DA_QK_W
    wvt = _transpose_cast(w_in[:, v0:v0 + DA_V_W])
    row = lambda width: pl.BlockSpec((tm, width), lambda i: (i, 0))
    nslab = DIL_W // LANES
    slabs = pl.BlockSpec((nslab, tm, LANES), lambda i: (0, i, 0))
    body = functools.partial(
        _inproj_body,
        q_scale=math.log2(math.e) / math.sqrt(DA_HEAD_DIM),
        qd_scale=1.0 / math.sqrt(DIL_HEAD_DIM))
    return pl.pallas_call(
        body,
        grid=(n // tm,),
        in_specs=[row(d), _resident((1, d)), row(cs.shape[1]), _resident(spread.shape),
                  _resident(w.shape), _resident(wvt.shape)],
        out_specs=[row(DA_QK_W), row(DA_QK_W),
                   pl.BlockSpec((None, None, DA_V_W, tm),
                                lambda i: (i // ns, (i % ns) // per_key_tile, 0, i % per_key_tile)),
                   slabs, slabs, slabs, row(d), row(d)],
        out_shape=[jax.ShapeDtypeStruct((n, DA_QK_W), _BF16),
                   jax.ShapeDtypeStruct((n, DA_QK_W), _BF16),
                   jax.ShapeDtypeStruct((batch, seq // tk, DA_V_W, tk), _BF16),
                   jax.ShapeDtypeStruct((nslab, n, LANES), _F32),
                   jax.ShapeDtypeStruct((nslab, n, LANES), _F32),
                   jax.ShapeDtypeStruct((nslab, n, LANES), _F32),
                   jax.ShapeDtypeStruct((n, d), _BF16),
                   jax.ShapeDtypeStruct((n, d), _BF16)],
        compiler_params=_params(1),
        name="inproj",
    )(x, g_pre.reshape(1, d), cs, spread, w, wvt)


def _diffattn_body(lam_ref, gsub_ref, q_ref, k_ref, vt_ref, o_ref,
                   s_ref, mx_ref, m_ref, acc_ref, fin_ref, *, tq, lambda_init):
    nk, _, tk = vt_ref.shape
    total = (q_ref.shape[0] // tq) * nk
    lane = lax.broadcasted_iota(jnp.int32, (tq, DA_V_DIM), 1)
    ones_rows = (lax.broadcasted_iota(jnp.int32, (BF16_ROWS, tk), 0) == 0).astype(_BF16)
    fin_ref[...] = jnp.ones(fin_ref.shape, _F32)

    def masked_q(t):
        q = q_ref[pl.ds(pl.multiple_of((t // nk) * tq, tq), tq), :]
        zero = jnp.zeros_like(q)
        return [jnp.where((lane >= c * DA_HEAD_DIM) & (lane < (c + 1) * DA_HEAD_DIM), q, zero)
                for c in range(2)]

    def score_chunk(qc, t, slot, c, kc):
        rows = pl.ds(pl.multiple_of((t % nk) * tk + kc * KEY_CHUNK, KEY_CHUNK), KEY_CHUNK)
        s = lax.dot_general(k_ref[rows, :], qc[c], (((1,), (1,)), ((), ())),
                            preferred_element_type=_F32)
        s_ref[slot, c, kc * KEY_CHUNK:(kc + 1) * KEY_CHUNK, :tq] = s
        return jnp.max(s, axis=0, keepdims=True)

    def scores(t, slot):
        qc = masked_q(t)
        for c in range(2):
            cm = [score_chunk(qc, t, slot, c, kc) for kc in range(tk // KEY_CHUNK)]
            mx_ref[slot, c] = functools.reduce(jnp.maximum, cm)

    def step(t, j, slot):
        t_next = jnp.minimum(t + 1, total - 1)
        qc = masked_q(t_next)
        vt = jnp.concatenate([vt_ref[j], ones_rows], axis=0)
        m_new, alpha = [], []
        for c in range(2):
            if j == 0:
                m_new.append(mx_ref[slot, c])
                alpha.append(None)
            else:
                m_old = m_ref[c]
                m_new.append(jnp.maximum(m_old, mx_ref[slot, c]))
                alpha.append(jnp.exp2(m_old - m_new[c]))
            m_ref[c] = m_new[c]
        cmax, pv = [None, None], [None, None]
        for kc in range(tk // KEY_CHUNK):
            keys = slice(kc * KEY_CHUNK, (kc + 1) * KEY_CHUNK)
            for c in range(2):
                cm = score_chunk(qc, t_next, 1 - slot, c, kc)
                cmax[c] = cm if cmax[c] is None else jnp.maximum(cmax[c], cm)
                p = jnp.exp2(s_ref[slot, c, keys, :tq] - m_new[c]).astype(_BF16)
                d = jnp.dot(vt[:, keys], p, preferred_element_type=_F32)
                pv[c] = d if pv[c] is None else pv[c] + d
        dst = fin_ref if j == nk - 1 else acc_ref
        for c in range(2):
            mx_ref[1 - slot, c] = cmax[c]
            dst[c] = pv[c] if j == 0 else alpha[c] * acc_ref[c] + pv[c]

    def finalize(qi):
        lq1, lk1, lq2, lk2 = (lam_ref[i:i + 1, :] for i in range(4))
        lam = (jnp.exp(jnp.sum(lq1 * lk1, axis=1, keepdims=True))
               - jnp.exp(jnp.sum(lq2 * lk2, axis=1, keepdims=True)) + lambda_init)
        num = [fin_ref[c, :DA_V_DIM, :] for c in range(2)]
        den = [fin_ref[c, DA_V_DIM:DA_V_DIM + 1, :] for c in range(2)]
        o = num[0] / den[0] - lam * (num[1] / den[1])
        ms = jnp.mean(o * o, axis=0, keepdims=True)
        y = o * lax.rsqrt(ms + EPS) * gsub_ref[...] * (1.0 - lambda_init)
        o_ref[pl.ds(pl.multiple_of(qi * tq, tq), tq), :] = y.T.astype(o_ref.dtype)

    scores(0, 0)

    def trip(qi, carry):
        finalize(jnp.maximum(qi - 1, 0))
        for j in range(nk):
            step(qi * nk + j, j, j % 2)
        return carry

    lax.fori_loop(0, total // nk, trip, 0)
    finalize(total // nk - 1)


def _diffattn(q, k, vt, lam_vecs, g_subln, lambda_init):
    b, s, _ = q.shape
    tq = min(ATT_TQ, s)
    nk, tk = vt.shape[1], vt.shape[3]
    assert nk % 2 == 0 and s % tq == 0 and tk % KEY_CHUNK == 0
    body = functools.partial(_diffattn_body, tq=tq, lambda_init=lambda_init)
    head = pl.BlockSpec((None, s, DA_V_DIM), lambda bi, h: (bi, 0, h))
    return pl.pallas_call(
        body,
        grid=(b, DA_HEADS),
        in_specs=[_resident(lam_vecs.shape), _resident((DA_V_DIM, 1)), head, head,
                  pl.BlockSpec((None, nk, DA_V_DIM, tk), lambda bi, h: (bi, 0, h, 0))],
        out_specs=head,
        out_shape=jax.ShapeDtypeStruct((b, s, DA_V_W), _BF16),
        scratch_shapes=[pltpu.VMEM((2, 2, tk, tq + LANES), _F32),
                        pltpu.VMEM((2, 2, 1, tq), _F32),
                        pltpu.VMEM((2, 1, tq), _F32),
                        pltpu.VMEM((2, DA_V_DIM + BF16_ROWS, tq), _F32),
                        pltpu.VMEM((2, DA_V_DIM + BF16_ROWS, tq), _F32)],
        compiler_params=_params(2),
        name="diffattn",
    )(lam_vecs, g_subln.reshape(DA_V_DIM, 1), q, k, vt)


def _dilattn_body(q_ref, kp_ref, km_ref, kn_ref, vp_ref, vm_ref, vn_ref, o_ref, lse_ref, *folded,
                  dil, half, seq_len, tiles_per_seq, fold):
    t = q_ref.shape[1]
    per_res = t // dil
    qb, win = 2 * half, 4 * half
    n_sub = per_res // qb
    nh = DIL_HEADS_PER_GROUP
    nslab = DIL_GROUP_W // LANES
    base = (pl.program_id(0) % tiles_per_seq) * per_res
    inner = dil // fold
    head_of_lane = lax.broadcasted_iota(jnp.int32, (qb, DIL_GROUP_W), 1) // DIL_HEAD_DIM
    r_iota = lax.broadcasted_iota(jnp.int32, (nh * qb, win), 0) % qb
    j = lax.broadcasted_iota(jnp.int32, (nh * qb, win), 1)
    band_bias = jnp.where((j >= r_iota) & (j <= r_iota + 2 * half), 0.0, NEG).astype(_F32)
    j_row = lax.broadcasted_iota(jnp.int32, (1, win), 1)

    def rows(start, count, stride=inner):
        return pl.ds(start, count, stride=stride) if stride > 1 else pl.ds(start, count)

    def gather(ref, start, count):
        return jnp.concatenate([ref[sl, rows(start, count), :] for sl in range(nslab)],
                               axis=1).astype(_BF16)

    def window(prev_ref, main_ref, next_ref, res, i, row0):
        if n_sub == 1:
            lo, hi = gather(prev_ref, res, half), gather(next_ref, res, half)
        else:
            lo_main = gather(main_ref, jnp.maximum(row0 - half * dil, res), half)
            lo = jnp.where(i == 0, gather(prev_ref, res, half), lo_main)
            hi_main = gather(main_ref,
                             jnp.minimum(row0 + qb * dil, (per_res - half) * dil + res), half)
            hi = jnp.where(i == n_sub - 1, gather(next_ref, res, half), hi_main)
        return jnp.concatenate([lo, gather(main_ref, row0, qb), hi], axis=0)

    if fold > 1:
        qf_ref, kf_ref, vf_ref, of_ref, lf_ref = folded
        tf, hf = t // fold, half * inner
        for a in range(fold):
            for sl in range(nslab):
                qf_ref[a, sl] = q_ref[sl, rows(a, tf, fold), :]
                for dst, (p_ref, m_ref, n_ref) in ((kf_ref, (kp_ref, km_ref, kn_ref)),
                                                   (vf_ref, (vp_ref, vm_ref, vn_ref))):
                    dst[a, sl, 0:hf] = p_ref[sl, rows(a, hf, fold), :]
                    dst[a, sl, hf:hf + tf] = m_ref[sl, rows(a, tf, fold), :]
                    dst[a, sl, hf + tf:hf + tf + hf] = n_ref[sl, rows(a, hf, fold), :]

    def block(idx, carry):
        res, i = idx // n_sub, idx % n_sub
        if fold > 1:
            a = res % fold
            row0 = i * (qb * inner) + res // fold
            qi = gather(qf_ref.at[a], row0, qb)
            kw = gather(kf_ref.at[a], row0, win)
            vw = gather(vf_ref.at[a], row0, win)
            o_dst, lse_dst = of_ref.at[a], lf_ref.at[a]
        else:
            row0 = i * (qb * dil) + res
            qi = gather(q_ref, row0, qb)
            kw = window(kp_ref, km_ref, kn_ref, res, i, row0)
            vw = window(vp_ref, vm_ref, vn_ref, res, i, row0)
            o_dst, lse_dst = o_ref, lse_ref
        kpos = base + (i * qb - half) + j_row
        in_seq = (kpos >= 0) & (kpos < seq_len)
        zero = jnp.zeros_like(qi)
        qs = jnp.concatenate([jnp.where(head_of_lane == h, qi, zero) for h in range(nh)], axis=0)
        s = lax.dot_general(qs, kw, (((1,), (1,)), ((), ())), preferred_element_type=_F32)
        s = jnp.where(in_seq, s + band_bias, NEG)
        m = jnp.max(s, axis=1, keepdims=True)
        e = jnp.exp(s - m)
        l = jnp.sum(e, axis=1, keepdims=True)
        pv = jnp.dot(e.astype(_BF16), vw, preferred_element_type=_F32) / l
        lse = m + jnp.log(l)
        o = jnp.zeros((qb, DIL_GROUP_W), _F32)
        lse_full = jnp.zeros((qb, DIL_GROUP_W), _F32)
        for h in range(nh):
            sel = head_of_lane == h
            o = jnp.where(sel, pv[h * qb:(h + 1) * qb], o)
            lse_full = jnp.where(sel, lse[h * qb:(h + 1) * qb], lse_full)
        for sl in range(nslab):
            o_dst[sl, rows(row0, qb), :] = o[:, sl * LANES:(sl + 1) * LANES]
            lse_dst[sl, rows(row0, qb), :] = lse_full[:, sl * LANES:(sl + 1) * LANES]
        return carry

    lax.fori_loop(0, dil * n_sub, block, 0, unroll=DIL_UNROLL)

    if fold > 1:
        for a in range(fold):
            for sl in range(nslab):
                o_ref[sl, rows(a, tf, fold), :] = of_ref[a, sl]
                lse_ref[sl, rows(a, tf, fold), :] = lf_ref[a, sl]


def _dilattn(qd, kd, vd, group, dil, half, seq):
    n = qd.shape[1]
    nslab = DIL_GROUP_W // LANES
    t = min(DIL_T, seq)
    halo = half * dil
    assert seq % t == 0 and t % (2 * half * dil) == 0 and t % halo == 0 and half % 8 == 0
    tiles_per_seq = seq // t
    per_tile = t // halo
    last = n // halo - 1
    main = pl.BlockSpec((nslab, t, LANES), lambda i: (group, i, 0))
    prev = pl.BlockSpec((nslab, halo, LANES),
                        lambda i: (group, jnp.maximum(i * per_tile - 1, 0), 0))
    nxt = pl.BlockSpec((nslab, halo, LANES),
                       lambda i: (group, jnp.minimum((i + 1) * per_tile, last), 0))
    out = pl.BlockSpec((nslab, t, LANES), lambda i: (0, i, 0))
    fold = max(dil // DIL_MAX_STRIDE, 1)
    assert dil % fold == 0 and halo % fold == 0
    body = functools.partial(_dilattn_body, dil=dil, half=half, seq_len=seq // dil,
                             tiles_per_seq=tiles_per_seq, fold=fold)
    folded = []
    if fold > 1:
        tile = pltpu.VMEM((fold, nslab, t // fold, LANES), _F32)
        with_halos = pltpu.VMEM((fold, nslab, (t + 2 * halo) // fold, LANES), _F32)
        folded = [tile, with_halos, with_halos, tile, tile]
    return pl.pallas_call(
        body,
        grid=(n // t,),
        in_specs=[main, prev, main, nxt, prev, main, nxt],
        out_specs=[out, out],
        out_shape=[jax.ShapeDtypeStruct((nslab, n, LANES), _F32)] * 2,
        scratch_shapes=folded,
        compiler_params=_params(1),
        name=f"dilattn_d{dil}",
    )(qd, kd, kd, kd, vd, vd, vd)


def _merge_body(x_ref, oa_ref, o0_ref, o1_ref, o2_ref, l0_ref, l1_ref, l2_ref, ga_ref, gb_ref,
                wa_ref, wb_ref, wo_ref, g_ref, out_ref):
    od_slabs = []
    for sl in range(DIL_GROUP_W // LANES):
        lses = (l0_ref[sl], l1_ref[sl], l2_ref[sl])
        outs = (o0_ref[sl], o1_ref[sl], o2_ref[sl])
        m = jnp.maximum(jnp.maximum(lses[0], lses[1]), lses[2])
        es = [jnp.exp(l - m) for l in lses]
        den = es[0] + es[1] + es[2]
        od_slabs.append((es[0] / den) * outs[0] + (es[1] / den) * outs[1]
                        + (es[2] / den) * outs[2])
    od = jnp.concatenate(od_slabs, axis=1)
    pa = jnp.dot(oa_ref[...], wa_ref[...], preferred_element_type=_F32)
    pb = jnp.dot(od.astype(_BF16), wb_ref[...], preferred_element_type=_F32)
    merged = (jax.nn.sigmoid(ga_ref[...].astype(_F32)) * pa
              + jax.nn.sigmoid(gb_ref[...].astype(_F32)) * pb)
    mo = jnp.dot(merged.astype(_BF16), wo_ref[...], preferred_element_type=_F32)
    out_ref[...] = x_ref[...] + _rms(mo, g_ref[...])


def _merge(x, oa, dil_outs, dil_lses, ga, gb, w_proj_a, w_proj_b, w_out, g_post):
    n, d = x.shape
    tm = min(TOK_TM, n)
    row = lambda width: pl.BlockSpec((tm, width), lambda i: (i, 0))
    slabs = pl.BlockSpec((DIL_GROUP_W // LANES, tm, LANES), lambda i: (0, i, 0))
    return pl.pallas_call(
        _merge_body,
        grid=(n // tm,),
        in_specs=[row(d), row(DA_V_W)] + [slabs] * 6 + [row(d), row(d),
                  _resident(w_proj_a.shape), _resident(w_proj_b.shape), _resident(w_out.shape),
                  _resident((1, d))],
        out_specs=row(d),
        out_shape=jax.ShapeDtypeStruct((n, d), _F32),
        compiler_params=_params(1),
        name="merge",
    )(x, oa, *dil_outs, *dil_lses, ga, gb, w_proj_a.astype(_BF16), w_proj_b.astype(_BF16),
      w_out.astype(_BF16), g_post.reshape(1, d))


def _rope_tables(positions):
    half = ROT_DIM // 2
    inv = ROPE_THETA ** (-(jnp.arange(0, ROT_DIM, 2, dtype=_F32) / ROT_DIM))
    ang = positions.astype(_F32).reshape(-1, 1) * inv
    cs = jnp.concatenate([jnp.cos(ang), jnp.sin(ang)], axis=1)
    terms = []
    for _ in range(3):
        head = lax.bitcast_convert_type(
            lax.bitcast_convert_type(cs, jnp.uint32) & jnp.uint32(0xFFFF0000), _F32)
        terms.append(head.astype(_BF16))
        cs = cs - head
    spread = np.zeros((ROT_DIM, 3 * LANES), np.float32)
    for lane in range(LANES):
        p = lane % DA_HEAD_DIM
        if p < ROT_DIM:
            spread[p % half, lane] = 1.0
        if half <= p < ROT_DIM:
            spread[p, LANES + lane] = 1.0
        if p < half:
            spread[half + p, 2 * LANES + lane] = -1.0
    return (jnp.concatenate(terms, axis=1),
            jnp.asarray(np.concatenate([spread] * 3, axis=0), dtype=_BF16))


def kernel(x, positions, w_in, lambda_q1, lambda_k1, lambda_q2, lambda_k2, g_subln, w_proj_a, w_proj_b, w_out, w_gu1, w_down1, w_gu2, w_down2, g_pre_ffn1, g_post_ffn1, g_pre_mix, g_post_mix, g_pre_ffn2, g_post_ffn2):
    b, s, d = x.shape
    n = b * s
    depth = w_in.shape[0]
    rope = _rope_tables(positions)
    xf = x.reshape(n, d)
    for l in range(depth):
        lambda_init = 0.8 - 0.6 * math.exp(-0.3 * l)
        xf = _ffn(xf, g_pre_ffn1[l], w_gu1[l], w_down1[l], g_post_ffn1[l])

        q, k, vt, qd, kd, vd, ga, gb = _inproj(xf, g_pre_mix[l], w_in[l], rope, b, s)
        lam_vecs = jnp.stack([lambda_q1[l], lambda_k1[l], lambda_q2[l], lambda_k2[l]], axis=0)
        oa = _diffattn(q.reshape(b, s, DA_QK_W), k.reshape(b, s, DA_QK_W), vt, lam_vecs,
                       g_subln[l], lambda_init)

        dil_outs, dil_lses = [], []
        for gi, (win, dil) in enumerate(DIL_PAIRS):
            o_g, lse_g = _dilattn(qd, kd, vd, gi, dil, win // (2 * dil), s)
            dil_outs.append(o_g)
            dil_lses.append(lse_g)

        xf = _merge(xf, oa.reshape(n, DA_V_W), dil_outs, dil_lses, ga, gb,
                    w_proj_a[l], w_proj_b[l], w_out[l], g_post_mix[l])
        xf = _ffn(xf, g_pre_ffn2[l], w_gu2[l], w_down2[l], g_post_ffn2[l])
    return xf.reshape(b, s, d)
```

```python
import functools
import math

import jax
import jax.numpy as jnp
import numpy as np
from jax import lax
from jax.experimental import pallas as pl
from jax.experimental.pallas import tpu as pltpu

D_MODEL = 1024
DA_HEADS = 8
DA_HEAD_DIM = 64
DA_V_DIM = 2 * DA_HEAD_DIM
DA_QK_W = DA_HEADS * 2 * DA_HEAD_DIM
DA_V_W = DA_HEADS * DA_V_DIM
DIL_PAIRS = ((128, 1), (512, 4), (2048, 16))
DIL_HEADS_PER_GROUP = 4
DIL_HEAD_DIM = 64
DIL_GROUP_W = DIL_HEADS_PER_GROUP * DIL_HEAD_DIM
DIL_W = DIL_GROUP_W * len(DIL_PAIRS)
ROPE_THETA = 500000.0
ROT_DIM = 16
D_FF = 2816
EPS = 1e-6
NEG = -1e30

LANES = 128
BF16_ROWS = 16
VMEM_LIMIT = 56 * 1024 * 1024

FFN_TM = 512
FFN_CHUNKS = 2
TOK_TM = 512
ATT_TQ = 512
ATT_TK = 2048
KEY_CHUNK = 256
DIL_T = 2048
DIL_MAX_STRIDE = 4

_BF16 = jnp.bfloat16
_F32 = jnp.float32


def _params(n_axes):
    return pltpu.CompilerParams(dimension_semantics=("arbitrary",) * n_axes,
                                vmem_limit_bytes=VMEM_LIMIT)


def _resident(shape):
    zeros = (0,) * len(shape)
    return pl.BlockSpec(shape, lambda *_: zeros, pipeline_mode=pl.Buffered(1))


def _rms(x, g):
    ms = jnp.mean(x * x, axis=-1, keepdims=True)
    return x * lax.rsqrt(ms + EPS) * g


def _ffn_body(x_ref, gpre_ref, wgu_ref, wd_ref, gpost_ref, o_ref):
    rows = x_ref.shape[0] // FFN_CHUNKS
    for i in range(FFN_CHUNKS):
        x = x_ref[i * rows:(i + 1) * rows, :]
        h = _rms(x, gpre_ref[...]).astype(_BF16)
        gu = jnp.dot(h, wgu_ref[...], preferred_element_type=_F32)
        g = gu[:, :D_FF]
        u = gu[:, D_FF:]
        a = (g * jax.nn.sigmoid(g) * u).astype(_BF16)
        y = jnp.dot(a, wd_ref[...], preferred_element_type=_F32)
        o_ref[i * rows:(i + 1) * rows, :] = x + 0.5 * _rms(y, gpost_ref[...])


def _ffn(x, g_pre, w_gu, w_down, g_post):
    n, d = x.shape
    tm = min(FFN_TM, n)
    row = pl.BlockSpec((tm, d), lambda i: (i, 0))
    return pl.pallas_call(
        _ffn_body,
        grid=(n // tm,),
        in_specs=[row, _resident((1, d)), _resident(w_gu.shape), _resident(w_down.shape),
                  _resident((1, d))],
        out_specs=row,
        out_shape=jax.ShapeDtypeStruct((n, d), _F32),
        compiler_params=_params(1),
        name="ffn",
    )(x, g_pre.reshape(1, d), w_gu.astype(_BF16), w_down.astype(_BF16), g_post.reshape(1, d))


def _inproj_body(x_ref, g_ref, cs_ref, spread_ref, w_ref, wvt_ref,
                 q_ref, k_ref, vt_ref, qd_ref, kd_ref, vd_ref, ga_ref, gb_ref, *, q_scale, qd_scale):
    h = _rms(x_ref[...], g_ref[...]).astype(_BF16)
    pats = jnp.dot(cs_ref[...], spread_ref[...], preferred_element_type=_F32)
    lane = lax.broadcasted_iota(jnp.int32, (1, LANES), 1)
    cos = pats[:, :LANES] + (lane % DA_HEAD_DIM >= ROT_DIM).astype(_F32)
    sa = pats[:, LANES:2 * LANES]
    sb = pats[:, 2 * LANES:]

    def proj(start, width):
        return jnp.dot(h, w_ref[:, start:start + width], preferred_element_type=_F32)

    def rope(zj):
        return (zj * cos + pltpu.roll(zj, ROT_DIM // 2, 1) * sa
                + pltpu.roll(zj, LANES - ROT_DIM // 2, 1) * sb)

    def rope_store(z, out_ref, scale):
        for j in range(z.shape[1] // LANES):
            r = rope(z[:, j * LANES:(j + 1) * LANES]) * scale
            out_ref[:, j * LANES:(j + 1) * LANES] = r.astype(out_ref.dtype)

    def slab_store(z, out_ref, fn):
        for j in range(z.shape[1] // LANES):
            out_ref[j] = fn(z[:, j * LANES:(j + 1) * LANES])

    c = 0
    rope_store(proj(c, DA_QK_W), q_ref, q_scale); c += DA_QK_W
    rope_store(proj(c, DA_QK_W), k_ref, 1.0); c += DA_QK_W
    vt_ref[...] = lax.dot_general(wvt_ref[...], h, (((1,), (1,)), ((), ())),
                                  preferred_element_type=_F32).astype(vt_ref.dtype)
    c += DA_V_W
    slab_store(proj(c, DIL_W), qd_ref, lambda zj: rope(zj) * qd_scale); c += DIL_W
    slab_store(proj(c, DIL_W), kd_ref, rope); c += DIL_W
    slab_store(proj(c, DIL_W), vd_ref, lambda zj: zj); c += DIL_W
    ga_ref[...] = proj(c, D_MODEL).astype(ga_ref.dtype); c += D_MODEL
    gb_ref[...] = proj(c, D_MODEL).astype(gb_ref.dtype)


def _transpose_cast_body(x_ref, o_ref):
    o_ref[...] = x_ref[...].T.astype(o_ref.dtype)


def _transpose_cast(x):
    rows, cols = x.shape
    tc = min(2 * LANES, cols)
    return pl.pallas_call(
        _transpose_cast_body,
        grid=(cols // tc,),
        in_specs=[pl.BlockSpec((rows, tc), lambda i: (0, i))],
        out_specs=pl.BlockSpec((tc, rows), lambda i: (i, 0)),
        out_shape=jax.ShapeDtypeStruct((cols, rows), _BF16),
        compiler_params=_params(1),
        name="wv_transpose",
    )(x)


def _inproj(x, g_pre, w_in, rope, batch, seq):
    n, d = x.shape
    tm = min(TOK_TM, seq)
    ns = seq // tm
    tk = min(ATT_TK, seq)
    per_key_tile = tk // tm
    cs, spread = rope
    w = w_in.astype(_BF16)
    v0 = 2 * DA_QK_W
    wvt = _transpose_cast(w_in[:, v0:v0 + DA_V_W])
    row = lambda width: pl.BlockSpec((tm, width), lambda i: (i, 0))
    nslab = DIL_W // LANES
    slabs = pl.BlockSpec((nslab, tm, LANES), lambda i: (0, i, 0))
    body = functools.partial(
        _inproj_body,
        q_scale=math.log2(math.e) / math.sqrt(DA_HEAD_DIM),
        qd_scale=1.0 / math.sqrt(DIL_HEAD_DIM))
    return pl.pallas_call(
        body,
        grid=(n // tm,),
        in_specs=[row(d), _resident((1, d)), row(cs.shape[1]), _resident(spread.shape),
                  _resident(w.shape), _resident(wvt.shape)],
        out_specs=[row(DA_QK_W), row(DA_QK_W),
                   pl.BlockSpec((None, None, DA_V_W, tm),
                                lambda i: (i // ns, (i % ns) // per_key_tile, 0, i % per_key_tile)),
                   slabs, slabs, slabs, row(d), row(d)],
        out_shape=[jax.ShapeDtypeStruct((n, DA_QK_W), _BF16),
                   jax.ShapeDtypeStruct((n, DA_QK_W), _BF16),
                   jax.ShapeDtypeStruct((batch, seq // tk, DA_V_W, tk), _BF16),
                   jax.ShapeDtypeStruct((nslab, n, LANES), _F32),
                   jax.ShapeDtypeStruct((nslab, n, LANES), _F32),
                   jax.ShapeDtypeStruct((nslab, n, LANES), _F32),
                   jax.ShapeDtypeStruct((n, d), _BF16),
                   jax.ShapeDtypeStruct((n, d), _BF16)],
        compiler_params=_params(1),
        name="inproj",
    )(x, g_pre.reshape(1, d), cs, spread, w, wvt)


def _diffattn_body(lam_ref, gsub_ref, q_ref, k_ref, vt_ref, o_ref,
                   s_ref, mx_ref, m_ref, acc_ref, fin_ref, *, tq, lambda_init):
    nk, _, tk = vt_ref.shape
    total = (q_ref.shape[0] // tq) * nk
    lane = lax.broadcasted_iota(jnp.int32, (tq, DA_V_DIM), 1)
    ones_rows = (lax.broadcasted_iota(jnp.int32, (BF16_ROWS, tk), 0) == 0).astype(_BF16)
    fin_ref[...] = jnp.ones(fin_ref.shape, _F32)

    def masked_q(t):
        q = q_ref[pl.ds(pl.multiple_of((t // nk) * tq, tq), tq), :]
        zero = jnp.zeros_like(q)
        return [jnp.where((lane >= c * DA_HEAD_DIM) & (lane < (c + 1) * DA_HEAD_DIM), q, zero)
                for c in range(2)]

    def score_chunk(qc, t, slot, c, kc):
        rows = pl.ds(pl.multiple_of((t % nk) * tk + kc * KEY_CHUNK, KEY_CHUNK), KEY_CHUNK)
        s = lax.dot_general(k_ref[rows, :], qc[c], (((1,), (1,)), ((), ())),
                            preferred_element_type=_F32)
        s_ref[slot, c, kc * KEY_CHUNK:(kc + 1) * KEY_CHUNK, :tq] = s
        return jnp.max(s, axis=0, keepdims=True)

    def scores(t, slot):
        qc = masked_q(t)
        for c in range(2):
            cm = [score_chunk(qc, t, slot, c, kc) for kc in range(tk // KEY_CHUNK)]
            mx_ref[slot, c] = functools.reduce(jnp.maximum, cm)

    def step(t, j, slot):
        t_next = jnp.minimum(t + 1, total - 1)
        qc = masked_q(t_next)
        vt = jnp.concatenate([vt_ref[j], ones_rows], axis=0)
        m_new, alpha = [], []
        for c in range(2):
            if j == 0:
                m_new.append(mx_ref[slot, c])
                alpha.append(None)
            else:
                m_old = m_ref[c]
                m_new.append(jnp.maximum(m_old, mx_ref[slot, c]))
                alpha.append(jnp.exp2(m_old - m_new[c]))
            m_ref[c] = m_new[c]
        cmax, pv = [None, None], [None, None]
        for kc in range(tk // KEY_CHUNK):
            keys = slice(kc * KEY_CHUNK, (kc + 1) * KEY_CHUNK)
            for c in range(2):
                cm = score_chunk(qc, t_next, 1 - slot, c, kc)
                cmax[c] = cm if cmax[c] is None else jnp.maximum(cmax[c], cm)
                p = jnp.exp2(s_ref[slot, c, keys, :tq] - m_new[c]).astype(_BF16)
                d = jnp.dot(vt[:, keys], p, preferred_element_type=_F32)
                pv[c] = d if pv[c] is None else pv[c] + d
        dst = fin_ref if j == nk - 1 else acc_ref
        for c in range(2):
            mx_ref[1 - slot, c] = cmax[c]
            dst[c] = pv[c] if j == 0 else alpha[c] * acc_ref[c] + pv[c]

    def finalize(qi):
        lq1, lk1, lq2, lk2 = (lam_ref[i:i + 1, :] for i in range(4))
        lam = (jnp.exp(jnp.sum(lq1 * lk1, axis=1, keepdims=True))
               - jnp.exp(jnp.sum(lq2 * lk2, axis=1, keepdims=True)) + lambda_init)
        num = [fin_ref[c, :DA_V_DIM, :] for c in range(2)]
        den = [fin_ref[c, DA_V_DIM:DA_V_DIM + 1, :] for c in range(2)]
        o = num[0] / den[0] - lam * (num[1] / den[1])
        ms = jnp.mean(o * o, axis=0, keepdims=True)
        y = o * lax.rsqrt(ms + EPS) * gsub_ref[...] * (1.0 - lambda_init)
        o_ref[pl.ds(pl.multiple_of(qi * tq, tq), tq), :] = y.T.astype(o_ref.dtype)

    scores(0, 0)

    def trip(qi, carry):
        finalize(jnp.maximum(qi - 1, 0))
        for j in range(nk):
            step(qi * nk + j, j, j % 2)
        return carry

    lax.fori_loop(0, total // nk, trip, 0)
    finalize(total // nk - 1)


def _diffattn(q, k, vt, lam_vecs, g_subln, lambda_init):
    b, s, _ = q.shape
    tq = min(ATT_TQ, s)
    nk, tk = vt.shape[1], vt.shape[3]
    assert nk % 2 == 0 and s % tq == 0 and tk % KEY_CHUNK == 0
    body = functools.partial(_diffattn_body, tq=tq, lambda_init=lambda_init)
    head = pl.BlockSpec((None, s, DA_V_DIM), lambda bi, h: (bi, 0, h))
    return pl.pallas_call(
        body,
        grid=(b, DA_HEADS),
        in_specs=[_resident(lam_vecs.shape), _resident((DA_V_DIM, 1)), head, head,
                  pl.BlockSpec((None, nk, DA_V_DIM, tk), lambda bi, h: (bi, 0, h, 0))],
        out_specs=head,
        out_shape=jax.ShapeDtypeStruct((b, s, DA_V_W), _BF16),
        scratch_shapes=[pltpu.VMEM((2, 2, tk, tq + LANES), _F32),
                        pltpu.VMEM((2, 2, 1, tq), _F32),
                        pltpu.VMEM((2, 1, tq), _F32),
                        pltpu.VMEM((2, DA_V_DIM + BF16_ROWS, tq), _F32),
                        pltpu.VMEM((2, DA_V_DIM + BF16_ROWS, tq), _F32)],
        compiler_params=_params(2),
        name="diffattn",
    )(lam_vecs, g_subln.reshape(DA_V_DIM, 1), q, k, vt)


def _dilattn_body(q_ref, kp_ref, km_ref, kn_ref, vp_ref, vm_ref, vn_ref, o_ref, lse_ref, *folded,
                  dil, half, seq_len, tiles_per_seq, fold):
    t = q_ref.shape[1]
    per_res = t // dil
    qb, win = 2 * half, 4 * half
    n_sub = per_res // qb
    nh = DIL_HEADS_PER_GROUP
    nslab = DIL_GROUP_W // LANES
    base = (pl.program_id(0) % tiles_per_seq) * per_res
    inner = dil // fold
    head_of_lane = lax.broadcasted_iota(jnp.int32, (qb, DIL_GROUP_W), 1) // DIL_HEAD_DIM
    r_iota = lax.broadcasted_iota(jnp.int32, (nh * qb, win), 0) % qb
    j = lax.broadcasted_iota(jnp.int32, (nh * qb, win), 1)
    band_bias = jnp.where((j >= r_iota) & (j <= r_iota + 2 * half), 0.0, NEG).astype(_F32)
    j_row = lax.broadcasted_iota(jnp.int32, (1, win), 1)

    def rows(start, count, stride=inner):
        return pl.ds(start, count, stride=stride) if stride > 1 else pl.ds(start, count)

    def gather(ref, start, count):
        return jnp.concatenate([ref[sl, rows(start, count), :] for sl in range(nslab)],
                               axis=1).astype(_BF16)

    def window(prev_ref, main_ref, next_ref, res, i, row0):
        lo = (gather(prev_ref, res, half) if i == 0
              else gather(main_ref, row0 - half * dil, half))
        hi = (gather(next_ref, res, half) if i == n_sub - 1
              else gather(main_ref, row0 + qb * dil, half))
        return jnp.concatenate([lo, gather(main_ref, row0, qb), hi], axis=0)

    if fold > 1:
        qf_ref, kf_ref, vf_ref, of_ref, lf_ref = folded
        tf, hf = t // fold, half * inner
        for a in range(fold):
            for sl in range(nslab):
                qf_ref[a, sl] = q_ref[sl, rows(a, tf, fold), :]
                for dst, (p_ref, m_ref, n_ref) in ((kf_ref, (kp_ref, km_ref, kn_ref)),
                                                   (vf_ref, (vp_ref, vm_ref, vn_ref))):
                    dst[a, sl, 0:hf] = p_ref[sl, rows(a, hf, fold), :]
                    dst[a, sl, hf:hf + tf] = m_ref[sl, rows(a, tf, fold), :]
                    dst[a, sl, hf + tf:hf + tf + hf] = n_ref[sl, rows(a, hf, fold), :]

    def block(idx, carry):
        res, i = idx // n_sub, idx % n_sub
        if fold > 1:
            a = res % fold
            row0 = i * (qb * inner) + res // fold
            qi = gather(qf_ref.at[a], row0, qb)
            kw = gather(kf_ref.at[a], row0, win)
            vw = gather(vf_ref.at[a], row0, win)
            o_dst, lse_dst = of_ref.at[a], lf_ref.at[a]
        else:
            row0 = i * (qb * dil) + res
            qi = gather(q_ref, row0, qb)
            kw = window(kp_ref, km_ref, kn_ref, res, i, row0)
            vw = window(vp_ref, vm_ref, vn_ref, res, i, row0)
            o_dst, lse_dst = o_ref, lse_ref
        s_bias = band_bias
        if i == 0 or i == n_sub - 1:
            kpos = base + (i * qb - half) + j_row
            in_seq = (kpos >= 0) & (kpos < seq_len)
            s_bias = jnp.where(in_seq, band_bias, NEG)
        zero = jnp.zeros_like(qi)
        qs = jnp.concatenate([jnp.where(head_of_lane == h, qi, zero) for h in range(nh)], axis=0)
        s = lax.dot_general(qs, kw, (((1,), (1,)), ((), ())), preferred_element_type=_F32)
        s = s + s_bias
        m = jnp.max(s, axis=1, keepdims=True)
        e = jnp.exp(s - m)
        l = jnp.sum(e, axis=1, keepdims=True)
        pv = jnp.dot(e.astype(_BF16), vw, preferred_element_type=_F32) / l
        lse = m + jnp.log(l)
        o = jnp.zeros((qb, DIL_GROUP_W), _F32)
        lse_full = jnp.zeros((qb, DIL_GROUP_W), _F32)
        for h in range(nh):
            sel = head_of_lane == h
            o = jnp.where(sel, pv[h * qb:(h + 1) * qb], o)
            lse_full = jnp.where(sel, lse[h * qb:(h + 1) * qb], lse_full)
        for sl in range(nslab):
            o_dst[sl, rows(row0, qb), :] = o[:, sl * LANES:(sl + 1) * LANES]
            lse_dst[sl, rows(row0, qb), :] = lse_full[:, sl * LANES:(sl + 1) * LANES]
        return carry

    for idx in range(dil * n_sub):
        block(idx, None)

    if fold > 1:
        for a in range(fold):
            for sl in range(nslab):
                o_ref[sl, rows(a, tf, fold), :] = of_ref[a, sl]
                lse_ref[sl, rows(a, tf, fold), :] = lf_ref[a, sl]


def _dilattn(qd, kd, vd, group, dil, half, seq):
    nslab_all, n, _ = qd.shape
    nslab = DIL_GROUP_W // LANES
    t = min(DIL_T, seq)
    halo = half * dil
    assert seq % t == 0 and t % (2 * half * dil) == 0 and t % halo == 0 and half % 8 == 0
    tiles_per_seq = seq // t
    per_tile = t // halo
    last = n // halo - 1
    main = pl.BlockSpec((nslab, t, LANES), lambda i: (group, i, 0))
    prev = pl.BlockSpec((nslab, halo, LANES),
                        lambda i: (group, jnp.maximum(i * per_tile - 1, 0), 0))
    nxt = pl.BlockSpec((nslab, halo, LANES),
                       lambda i: (group, jnp.minimum((i + 1) * per_tile, last), 0))
    out = pl.BlockSpec((nslab, t, LANES), lambda i: (0, i, 0))
    fold = max(dil // DIL_MAX_STRIDE, 1)
    assert dil % fold == 0 and halo % fold == 0
    body = functools.partial(_dilattn_body, dil=dil, half=half, seq_len=seq // dil,
                             tiles_per_seq=tiles_per_seq, fold=fold)
    folded = []
    if fold > 1:
        tile = pltpu.VMEM((fold, nslab, t // fold, LANES), _F32)
        with_halos = pltpu.VMEM((fold, nslab, (t + 2 * halo) // fold, LANES), _F32)
        folded = [tile, with_halos, with_halos, tile, tile]
    return pl.pallas_call(
        body,
        grid=(n // t,),
        in_specs=[main, prev, main, nxt, prev, main, nxt],
        out_specs=[out, out],
        out_shape=[jax.ShapeDtypeStruct((nslab, n, LANES), _F32)] * 2,
        scratch_shapes=folded,
        compiler_params=_params(1),
        name=f"dilattn_d{dil}",
    )(qd, kd, kd, kd, vd, vd, vd)


def _merge_body(x_ref, oa_ref, o0_ref, o1_ref, o2_ref, l0_ref, l1_ref, l2_ref, ga_ref, gb_ref,
                wa_ref, wb_ref, wo_ref, g_ref, out_ref):
    od_slabs = []
    for sl in range(DIL_GROUP_W // LANES):
        lses = (l0_ref[sl], l1_ref[sl], l2_ref[sl])
        outs = (o0_ref[sl], o1_ref[sl], o2_ref[sl])
        m = jnp.maximum(jnp.maximum(lses[0], lses[1]), lses[2])
        es = [jnp.exp(l - m) for l in lses]
        den = es[0] + es[1] + es[2]
        od_slabs.append((es[0] / den) * outs[0] + (es[1] / den) * outs[1]
                        + (es[2] / den) * outs[2])
    od = jnp.concatenate(od_slabs, axis=1)
    pa = jnp.dot(oa_ref[...], wa_ref[...], preferred_element_type=_F32)
    pb = jnp.dot(od.astype(_BF16), wb_ref[...], preferred_element_type=_F32)
    merged = (jax.nn.sigmoid(ga_ref[...].astype(_F32)) * pa
              + jax.nn.sigmoid(gb_ref[...].astype(_F32)) * pb)
    mo = jnp.dot(merged.astype(_BF16), wo_ref[...], preferred_element_type=_F32)
    out_ref[...] = x_ref[...] + _rms(mo, g_ref[...])


def _merge(x, oa, dil_outs, dil_lses, ga, gb, w_proj_a, w_proj_b, w_out, g_post):
    n, d = x.shape
    tm = min(TOK_TM, n)
    row = lambda width: pl.BlockSpec((tm, width), lambda i: (i, 0))
    slabs = pl.BlockSpec((DIL_GROUP_W // LANES, tm, LANES), lambda i: (0, i, 0))
    return pl.pallas_call(
        _merge_body,
        grid=(n // tm,),
        in_specs=[row(d), row(DA_V_W)] + [slabs] * 6 + [row(d), row(d),
                  _resident(w_proj_a.shape), _resident(w_proj_b.shape), _resident(w_out.shape),
                  _resident((1, d))],
        out_specs=row(d),
        out_shape=jax.ShapeDtypeStruct((n, d), _F32),
        compiler_params=_params(1),
        name="merge",
    )(x, oa, *dil_outs, *dil_lses, ga, gb, w_proj_a.astype(_BF16), w_proj_b.astype(_BF16),
      w_out.astype(_BF16), g_post.reshape(1, d))


def _rope_tables(positions):
    half = ROT_DIM // 2
    inv = ROPE_THETA ** (-(jnp.arange(0, ROT_DIM, 2, dtype=_F32) / ROT_DIM))
    ang = positions.astype(_F32).reshape(-1, 1) * inv
    cs = jnp.concatenate([jnp.cos(ang), jnp.sin(ang)], axis=1)
    terms = []
    for _ in range(3):
        head = lax.bitcast_convert_type(
            lax.bitcast_convert_type(cs, jnp.uint32) & jnp.uint32(0xFFFF0000), _F32)
        terms.append(head.astype(_BF16))
        cs = cs - head
    spread = np.zeros((ROT_DIM, 3 * LANES), np.float32)
    for lane in range(LANES):
        p = lane % DA_HEAD_DIM
        if p < ROT_DIM:
            spread[p % half, lane] = 1.0
        if half <= p < ROT_DIM:
            spread[p, LANES + lane] = 1.0
        if p < half:
            spread[half + p, 2 * LANES + lane] = -1.0
    return (jnp.concatenate(terms, axis=1),
            jnp.asarray(np.concatenate([spread] * 3, axis=0), dtype=_BF16))


def kernel(x, positions, w_in, lambda_q1, lambda_k1, lambda_q2, lambda_k2, g_subln, w_proj_a, w_proj_b, w_out, w_gu1, w_down1, w_gu2, w_down2, g_pre_ffn1, g_post_ffn1, g_pre_mix, g_post_mix, g_pre_ffn2, g_post_ffn2):
    b, s, d = x.shape
    n = b * s
    depth = w_in.shape[0]
    rope = _rope_tables(positions)
    xf = x.reshape(n, d)
    for l in range(depth):
        lambda_init = 0.8 - 0.6 * math.exp(-0.3 * l)
        xf = _ffn(xf, g_pre_ffn1[l], w_gu1[l], w_down1[l], g_post_ffn1[l])

        q, k, vt, qd, kd, vd, ga, gb = _inproj(xf, g_pre_mix[l], w_in[l], rope, b, s)
        lam_vecs = jnp.stack([lambda_q1[l], lambda_k1[l], lambda_q2[l], lambda_k2[l]], axis=0)
        oa = _diffattn(q.reshape(b, s, DA_QK_W), k.reshape(b, s, DA_QK_W), vt, lam_vecs,
                       g_subln[l], lambda_init)

        dil_outs, dil_lses = [], []
        for gi, (win, dil) in enumerate(DIL_PAIRS):
            o_g, lse_g = _dilattn(qd, kd, vd, gi, dil, win // (2 * dil), s)
            dil_outs.append(o_g)
            dil_lses.append(lse_g)

        xf = _merge(xf, oa.reshape(n, DA_V_W), dil_outs, dil_lses, ga, gb,
                    w_proj_a[l], w_proj_b[l], w_out[l], g_post_mix[l])
        xf = _ffn(xf, g_pre_ffn2[l], w_gu2[l], w_down2[l], g_post_ffn2[l])
    return xf.reshape(b, s, d)
```

```python
import functools
import math

import jax
import jax.numpy as jnp
import numpy as np
from jax import lax
from jax.experimental import pallas as pl
from jax.experimental.pallas import tpu as pltpu

D_MODEL = 1024
DA_HEADS = 8
DA_HEAD_DIM = 64
DA_V_DIM = 2 * DA_HEAD_DIM
DA_QK_W = DA_HEADS * 2 * DA_HEAD_DIM
DA_V_W = DA_HEADS * DA_V_DIM
DIL_PAIRS = ((128, 1), (512, 4), (2048, 16))
DIL_HEADS_PER_GROUP = 4
DIL_HEAD_DIM = 64
DIL_GROUP_W = DIL_HEADS_PER_GROUP * DIL_HEAD_DIM
DIL_W = DIL_GROUP_W * len(DIL_PAIRS)
ROPE_THETA = 500000.0
ROT_DIM = 16
D_FF = 2816
EPS = 1e-6
NEG = -1e30

LANES = 128
BF16_ROWS = 16
VMEM_LIMIT = 56 * 1024 * 1024

FFN_TM = 512
FFN_CHUNKS = 2
TOK_TM = 512
ATT_TQ = 512
ATT_TK = 2048
KEY_CHUNK = 256
DIL_T = 2048
DIL_MAX_STRIDE = 4

_BF16 = jnp.bfloat16
_F32 = jnp.float32


def _params(n_axes):
    return pltpu.CompilerParams(dimension_semantics=("arbitrary",) * n_axes,
                                vmem_limit_bytes=VMEM_LIMIT)


def _resident(shape):
    zeros = (0,) * len(shape)
    return pl.BlockSpec(shape, lambda *_: zeros, pipeline_mode=pl.Buffered(1))


def _rms(x, g):
    ms = jnp.mean(x * x, axis=-1, keepdims=True)
    return x * lax.rsqrt(ms + EPS) * g


def _ffn_body(x_ref, gpre_ref, wgu_ref, wd_ref, gpost_ref, o_ref):
    rows = x_ref.shape[0] // FFN_CHUNKS
    for i in range(FFN_CHUNKS):
        x = x_ref[i * rows:(i + 1) * rows, :]
        h = _rms(x, gpre_ref[...]).astype(_BF16)
        gu = jnp.dot(h, wgu_ref[...], preferred_element_type=_F32)
        g = gu[:, :D_FF]
        u = gu[:, D_FF:]
        a = (g * jax.nn.sigmoid(g) * u).astype(_BF16)
        y = jnp.dot(a, wd_ref[...], preferred_element_type=_F32)
        o_ref[i * rows:(i + 1) * rows, :] = x + 0.5 * _rms(y, gpost_ref[...])


def _ffn(x, g_pre, w_gu, w_down, g_post):
    n, d = x.shape
    tm = min(FFN_TM, n)
    row = pl.BlockSpec((tm, d), lambda i: (i, 0))
    return pl.pallas_call(
        _ffn_body,
        grid=(n // tm,),
        in_specs=[row, _resident((1, d)), _resident(w_gu.shape), _resident(w_down.shape),
                  _resident((1, d))],
        out_specs=row,
        out_shape=jax.ShapeDtypeStruct((n, d), _F32),
        compiler_params=_params(1),
        name="ffn",
    )(x, g_pre.reshape(1, d), w_gu.astype(_BF16), w_down.astype(_BF16), g_post.reshape(1, d))


def _inproj_body(x_ref, g_ref, cs_ref, spread_ref, w_ref, wvt_ref,
                 q_ref, k_ref, vt_ref, qd_ref, kd_ref, vd_ref, ga_ref, gb_ref, *, q_scale, qd_scale):
    h = _rms(x_ref[...], g_ref[...]).astype(_BF16)
    pats = jnp.dot(cs_ref[...], spread_ref[...], preferred_element_type=_F32)
    lane = lax.broadcasted_iota(jnp.int32, (1, LANES), 1)
    cos = pats[:, :LANES] + (lane % DA_HEAD_DIM >= ROT_DIM).astype(_F32)
    sa = pats[:, LANES:2 * LANES]
    sb = pats[:, 2 * LANES:]

    def proj(start, width):
        return jnp.dot(h, w_ref[:, start:start + width], preferred_element_type=_F32)

    def rope(zj):
        return (zj * cos + pltpu.roll(zj, ROT_DIM // 2, 1) * sa
                + pltpu.roll(zj, LANES - ROT_DIM // 2, 1) * sb)

    def rope_store(z, out_ref, scale):
        for j in range(z.shape[1] // LANES):
            r = rope(z[:, j * LANES:(j + 1) * LANES]) * scale
            out_ref[:, j * LANES:(j + 1) * LANES] = r.astype(out_ref.dtype)

    def slab_store(z, out_ref, fn):
        for j in range(z.shape[1] // LANES):
            out_ref[j] = fn(z[:, j * LANES:(j + 1) * LANES])

    c = 0
    rope_store(proj(c, DA_QK_W), q_ref, q_scale); c += DA_QK_W
    rope_store(proj(c, DA_QK_W), k_ref, 1.0); c += DA_QK_W
    vt_ref[...] = lax.dot_general(wvt_ref[...], h, (((1,), (1,)), ((), ())),
                                  preferred_element_type=_F32).astype(vt_ref.dtype)
    c += DA_V_W
    slab_store(proj(c, DIL_W), qd_ref, lambda zj: rope(zj) * qd_scale); c += DIL_W
    slab_store(proj(c, DIL_W), kd_ref, rope); c += DIL_W
    slab_store(proj(c, DIL_W), vd_ref, lambda zj: zj); c += DIL_W
    ga_ref[...] = proj(c, D_MODEL).astype(ga_ref.dtype); c += D_MODEL
    gb_ref[...] = proj(c, D_MODEL).astype(gb_ref.dtype)


def _transpose_cast_body(x_ref, o_ref):
    o_ref[...] = x_ref[...].T.astype(o_ref.dtype)


def _transpose_cast(x):
    rows, cols = x.shape
    tc = min(2 * LANES, cols)
    return pl.pallas_call(
        _transpose_cast_body,
        grid=(cols // tc,),
        in_specs=[pl.BlockSpec((rows, tc), lambda i: (0, i))],
        out_specs=pl.BlockSpec((tc, rows), lambda i: (i, 0)),
        out_shape=jax.ShapeDtypeStruct((cols, rows), _BF16),
        compiler_params=_params(1),
        name="wv_transpose",
    )(x)


def _inproj(x, g_pre, w_in, rope, batch, seq):
    n, d = x.shape
    tm = min(TOK_TM, seq)
    ns = seq // tm
    tk = min(ATT_TK, seq)
    per_key_tile = tk // tm
    cs, spread = rope
    w = w_in.astype(_BF16)
    v0 = 2 * DA_QK_W
    wvt = _transpose_cast(w_in[:, v0:v0 + DA_V_W])
    row = lambda width: pl.BlockSpec((tm, width), lambda i: (i, 0))
    nslab = DIL_W // LANES
    slabs = pl.BlockSpec((nslab, tm, LANES), lambda i: (0, i, 0))
    body = functools.partial(
        _inproj_body,
        q_scale=math.log2(math.e) / math.sqrt(DA_HEAD_DIM),
        qd_scale=1.0 / math.sqrt(DIL_HEAD_DIM))
    return pl.pallas_call(
        body,
        grid=(n // tm,),
        in_specs=[row(d), _resident((1, d)), row(cs.shape[1]), _resident(spread.shape),
                  _resident(w.shape), _resident(wvt.shape)],
        out_specs=[row(DA_QK_W), row(DA_QK_W),
                   pl.BlockSpec((None, None, DA_V_W, tm),
                                lambda i: (i // ns, (i % ns) // per_key_tile, 0, i % per_key_tile)),
                   slabs, slabs, slabs, row(d), row(d)],
        out_shape=[jax.ShapeDtypeStruct((n, DA_QK_W), _BF16),
                   jax.ShapeDtypeStruct((n, DA_QK_W), _BF16),
                   jax.ShapeDtypeStruct((batch, seq // tk, DA_V_W, tk), _BF16),
                   jax.ShapeDtypeStruct((nslab, n, LANES), _F32),
                   jax.ShapeDtypeStruct((nslab, n, LANES), _F32),
                   jax.ShapeDtypeStruct((nslab, n, LANES), _F32),
                   jax.ShapeDtypeStruct((n, d), _BF16),
                   jax.ShapeDtypeStruct((n, d), _BF16)],
        compiler_params=_params(1),
        name="inproj",
    )(x, g_pre.reshape(1, d), cs, spread, w, wvt)


def _diffattn_body(lam_ref, gsub_ref, q_ref, k_ref, vt_ref, o_ref,
                   s_ref, mx_ref, m_ref, acc_ref, fin_ref, *, tq, lambda_init):
    nk, _, tk = vt_ref.shape
    total = (q_ref.shape[0] // tq) * nk
    lane = lax.broadcasted_iota(jnp.int32, (tq, DA_V_DIM), 1)
    ones_rows = (lax.broadcasted_iota(jnp.int32, (BF16_ROWS, tk), 0) == 0).astype(_BF16)
    fin_ref[...] = jnp.ones(fin_ref.shape, _F32)

    def masked_q(t):
        q = q_ref[pl.ds(pl.multiple_of((t // nk) * tq, tq), tq), :]
        zero = jnp.zeros_like(q)
        return [jnp.where((lane >= c * DA_HEAD_DIM) & (lane < (c + 1) * DA_HEAD_DIM), q, zero)
                for c in range(2)]

    def score_chunk(qc, t, slot, c, kc):
        rows = pl.ds(pl.multiple_of((t % nk) * tk + kc * KEY_CHUNK, KEY_CHUNK), KEY_CHUNK)
        s = lax.dot_general(k_ref[rows, :], qc[c], (((1,), (1,)), ((), ())),
                            preferred_element_type=_F32)
        s_ref[slot, c, kc * KEY_CHUNK:(kc + 1) * KEY_CHUNK, :tq] = s
        return jnp.max(s, axis=0, keepdims=True)

    def scores(t, slot):
        qc = masked_q(t)
        for c in range(2):
            cm = [score_chunk(qc, t, slot, c, kc) for kc in range(tk // KEY_CHUNK)]
            mx_ref[slot, c] = functools.reduce(jnp.maximum, cm)

    def step(t, j, slot):
        t_next = jnp.minimum(t + 1, total - 1)
        qc = masked_q(t_next)
        vt = jnp.concatenate([vt_ref[j], ones_rows], axis=0)
        m_new, alpha = [], []
        for c in range(2):
            if j == 0:
                m_new.append(mx_ref[slot, c])
                alpha.append(None)
            else:
                m_old = m_ref[c]
                m_new.append(jnp.maximum(m_old, mx_ref[slot, c]))
                alpha.append(jnp.exp2(m_old - m_new[c]))
            m_ref[c] = m_new[c]
        cmax, pv = [None, None], [None, None]
        for kc in range(tk // KEY_CHUNK):
            keys = slice(kc * KEY_CHUNK, (kc + 1) * KEY_CHUNK)
            for c in range(2):
                p = jnp.exp2(s_ref[slot, c, keys, :tq] - m_new[c]).astype(_BF16)
                d = jnp.dot(vt[:, keys], p, preferred_element_type=_F32)
                pv[c] = d if pv[c] is None else pv[c] + d
                cm = score_chunk(qc, t_next, 1 - slot, c, kc)
                cmax[c] = cm if cmax[c] is None else jnp.maximum(cmax[c], cm)
        dst = fin_ref if j == nk - 1 else acc_ref
        for c in range(2):
            mx_ref[1 - slot, c] = cmax[c]
            dst[c] = pv[c] if j == 0 else alpha[c] * acc_ref[c] + pv[c]

    def finalize(qi):
        lq1, lk1, lq2, lk2 = (lam_ref[i:i + 1, :] for i in range(4))
        lam = (jnp.exp(jnp.sum(lq1 * lk1, axis=1, keepdims=True))
               - jnp.exp(jnp.sum(lq2 * lk2, axis=1, keepdims=True)) + lambda_init)
        num = [fin_ref[c, :DA_V_DIM, :] for c in range(2)]
        den = [fin_ref[c, DA_V_DIM:DA_V_DIM + 1, :] for c in range(2)]
        o = num[0] / den[0] - lam * (num[1] / den[1])
        ms = jnp.mean(o * o, axis=0, keepdims=True)
        y = o * lax.rsqrt(ms + EPS) * gsub_ref[...] * (1.0 - lambda_init)
        o_ref[pl.ds(pl.multiple_of(qi * tq, tq), tq), :] = y.T.astype(o_ref.dtype)

    scores(0, 0)

    def trip(qi, carry):
        finalize(jnp.maximum(qi - 1, 0))
        for j in range(nk):
            step(qi * nk + j, j, j % 2)
        return carry

    lax.fori_loop(0, total // nk, trip, 0)
    finalize(total // nk - 1)


def _diffattn(q, k, vt, lam_vecs, g_subln, lambda_init):
    b, s, _ = q.shape
    tq = min(ATT_TQ, s)
    nk, tk = vt.shape[1], vt.shape[3]
    assert nk % 2 == 0 and s % tq == 0 and tk % KEY_CHUNK == 0
    body = functools.partial(_diffattn_body, tq=tq, lambda_init=lambda_init)
    head = pl.BlockSpec((None, s, DA_V_DIM), lambda bi, h: (bi, 0, h))
    return pl.pallas_call(
        body,
        grid=(b, DA_HEADS),
        in_specs=[_resident(lam_vecs.shape), _resident((DA_V_DIM, 1)), head, head,
                  pl.BlockSpec((None, nk, DA_V_DIM, tk), lambda bi, h: (bi, 0, h, 0))],
        out_specs=head,
        out_shape=jax.ShapeDtypeStruct((b, s, DA_V_W), _BF16),
        scratch_shapes=[pltpu.VMEM((2, 2, tk, tq + LANES), _F32),
                        pltpu.VMEM((2, 2, 1, tq), _F32),
                        pltpu.VMEM((2, 1, tq), _F32),
                        pltpu.VMEM((2, DA_V_DIM + BF16_ROWS, tq), _F32),
                        pltpu.VMEM((2, DA_V_DIM + BF16_ROWS, tq), _F32)],
        compiler_params=_params(2),
        name="diffattn",
    )(lam_vecs, g_subln.reshape(DA_V_DIM, 1), q, k, vt)


def _dilattn_body(q_ref, kp_ref, km_ref, kn_ref, vp_ref, vm_ref, vn_ref, o_ref, lse_ref, *folded,
                  dil, half, seq_len, tiles_per_seq, fold):
    t = q_ref.shape[1]
    per_res = t // dil
    qb, win = 2 * half, 4 * half
    n_sub = per_res // qb
    nh = DIL_HEADS_PER_GROUP
    nslab = DIL_GROUP_W // LANES
    base = (pl.program_id(0) % tiles_per_seq) * per_res
    inner = dil // fold
    head_of_lane = lax.broadcasted_iota(jnp.int32, (qb, DIL_GROUP_W), 1) // DIL_HEAD_DIM
    r_iota = lax.broadcasted_iota(jnp.int32, (nh * qb, win), 0) % qb
    j = lax.broadcasted_iota(jnp.int32, (nh * qb, win), 1)
    band_bias = jnp.where((j >= r_iota) & (j <= r_iota + 2 * half), 0.0, NEG).astype(_F32)
    j_row = lax.broadcasted_iota(jnp.int32, (1, win), 1)

    def rows(start, count, stride=inner):
        return pl.ds(start, count, stride=stride) if stride > 1 else pl.ds(start, count)

    def gather(ref, start, count):
        return jnp.concatenate([ref[sl, rows(start, count), :] for sl in range(nslab)],
                               axis=1).astype(_BF16)

    def window(prev_ref, main_ref, next_ref, res, i, row0):
        lo = (gather(prev_ref, res, half) if i == 0
              else gather(main_ref, row0 - half * dil, half))
        hi = (gather(next_ref, res, half) if i == n_sub - 1
              else gather(main_ref, row0 + qb * dil, half))
        return jnp.concatenate([lo, gather(main_ref, row0, qb), hi], axis=0)

    if fold > 1:
        qf_ref, kf_ref, vf_ref, of_ref, lf_ref = folded
        tf, hf = t // fold, half * inner
        for a in range(fold):
            for sl in range(nslab):
                qf_ref[a, sl] = q_ref[sl, rows(a, tf, fold), :]
                for dst, (p_ref, m_ref, n_ref) in ((kf_ref, (kp_ref, km_ref, kn_ref)),
                                                   (vf_ref, (vp_ref, vm_ref, vn_ref))):
                    dst[a, sl, 0:hf] = p_ref[sl, rows(a, hf, fold), :]
                    dst[a, sl, hf:hf + tf] = m_ref[sl, rows(a, tf, fold), :]
                    dst[a, sl, hf + tf:hf + tf + hf] = n_ref[sl, rows(a, hf, fold), :]

    def block(idx, carry):
        res, i = idx // n_sub, idx % n_sub
        if fold > 1:
            a = res % fold
            row0 = i * (qb * inner) + res // fold
            qi = gather(qf_ref.at[a], row0, qb)
            kw = gather(kf_ref.at[a], row0, win)
            vw = gather(vf_ref.at[a], row0, win)
            o_dst, lse_dst = of_ref.at[a], lf_ref.at[a]
        else:
            row0 = i * (qb * dil) + res
            qi = gather(q_ref, row0, qb)
            kw = window(kp_ref, km_ref, kn_ref, res, i, row0)
            vw = window(vp_ref, vm_ref, vn_ref, res, i, row0)
            o_dst, lse_dst = o_ref, lse_ref
        s_bias = band_bias
        if i == 0 or i == n_sub - 1:
            kpos = base + (i * qb - half) + j_row
            in_seq = (kpos >= 0) & (kpos < seq_len)
            s_bias = jnp.where(in_seq, band_bias, NEG)
        zero = jnp.zeros_like(qi)
        qs = jnp.concatenate([jnp.where(head_of_lane == h, qi, zero) for h in range(nh)], axis=0)
        s = lax.dot_general(qs, kw, (((1,), (1,)), ((), ())), preferred_element_type=_F32)
        s = s + s_bias
        m = jnp.max(s, axis=1, keepdims=True)
        e = jnp.exp(s - m)
        l = jnp.sum(e, axis=1, keepdims=True)
        pv = jnp.dot(e.astype(_BF16), vw, preferred_element_type=_F32) / l
        lse = m + jnp.log(l)
        o = jnp.zeros((qb, DIL_GROUP_W), _F32)
        lse_full = jnp.zeros((qb, DIL_GROUP_W), _F32)
        for h in range(nh):
            sel = head_of_lane == h
            o = jnp.where(sel, pv[h * qb:(h + 1) * qb], o)
            lse_full = jnp.where(sel, lse[h * qb:(h + 1) * qb], lse_full)
        for sl in range(nslab):
            o_dst[sl, rows(row0, qb), :] = o[:, sl * LANES:(sl + 1) * LANES]
            lse_dst[sl, rows(row0, qb), :] = lse_full[:, sl * LANES:(sl + 1) * LANES]
        return carry

    for idx in range(dil * n_sub):
        block(idx, None)

    if fold > 1:
        for a in range(fold):
            for sl in range(nslab):
                o_ref[sl, rows(a, tf, fold), :] = of_ref[a, sl]
                lse_ref[sl, rows(a, tf, fold), :] = lf_ref[a, sl]


def _dilattn(qd, kd, vd, group, dil, half, seq):
    nslab_all, n, _ = qd.shape
    nslab = DIL_GROUP_W // LANES
    t = min(DIL_T, seq)
    halo = half * dil
    assert seq % t == 0 and t % (2 * half * dil) == 0 and t % halo == 0 and half % 8 == 0
    tiles_per_seq = seq // t
    per_tile = t // halo
    last = n // halo - 1
    main = pl.BlockSpec((nslab, t, LANES), lambda i: (group, i, 0))
    prev = pl.BlockSpec((nslab, halo, LANES),
                        lambda i: (group, jnp.maximum(i * per_tile - 1, 0), 0))
    nxt = pl.BlockSpec((nslab, halo, LANES),
                       lambda i: (group, jnp.minimum((i + 1) * per_tile, last), 0))
    out = pl.BlockSpec((nslab, t, LANES), lambda i: (0, i, 0))
    fold = max(dil // DIL_MAX_STRIDE, 1)
    assert dil % fold == 0 and halo % fold == 0
    body = functools.partial(_dilattn_body, dil=dil, half=half, seq_len=seq // dil,
                             tiles_per_seq=tiles_per_seq, fold=fold)
    folded = []
    if fold > 1:
        tile = pltpu.VMEM((fold, nslab, t // fold, LANES), _F32)
        with_halos = pltpu.VMEM((fold, nslab, (t + 2 * halo) // fold, LANES), _F32)
        folded = [tile, with_halos, with_halos, tile, tile]
    return pl.pallas_call(
        body,
        grid=(n // t,),
        in_specs=[main, prev, main, nxt, prev, main, nxt],
        out_specs=[out, out],
        out_shape=[jax.ShapeDtypeStruct((nslab, n, LANES), _F32)] * 2,
        scratch_shapes=folded,
        compiler_params=_params(1),
        name=f"dilattn_d{dil}",
    )(qd, kd, kd, kd, vd, vd, vd)


def _merge_body(x_ref, oa_ref, o0_ref, o1_ref, o2_ref, l0_ref, l1_ref, l2_ref, ga_ref, gb_ref,
                wa_ref, wb_ref, wo_ref, g_ref, out_ref):
    od_slabs = []
    for sl in range(DIL_GROUP_W // LANES):
        lses = (l0_ref[sl], l1_ref[sl], l2_ref[sl])
        outs = (o0_ref[sl], o1_ref[sl], o2_ref[sl])
        m = jnp.maximum(jnp.maximum(lses[0], lses[1]), lses[2])
        es = [jnp.exp(l - m) for l in lses]
        den = es[0] + es[1] + es[2]
        od_slabs.append((es[0] / den) * outs[0] + (es[1] / den) * outs[1]
                        + (es[2] / den) * outs[2])
    od = jnp.concatenate(od_slabs, axis=1)
    pa = jnp.dot(oa_ref[...], wa_ref[...], preferred_element_type=_F32)
    pb = jnp.dot(od.astype(_BF16), wb_ref[...], preferred_element_type=_F32)
    merged = (jax.nn.sigmoid(ga_ref[...].astype(_F32)) * pa
              + jax.nn.sigmoid(gb_ref[...].astype(_F32)) * pb)
    mo = jnp.dot(merged.astype(_BF16), wo_ref[...], preferred_element_type=_F32)
    out_ref[...] = x_ref[...] + _rms(mo, g_ref[...])


def _merge(x, oa, dil_outs, dil_lses, ga, gb, w_proj_a, w_proj_b, w_out, g_post):
    n, d = x.shape
    tm = min(TOK_TM, n)
    row = lambda width: pl.BlockSpec((tm, width), lambda i: (i, 0))
    slabs = pl.BlockSpec((DIL_GROUP_W // LANES, tm, LANES), lambda i: (0, i, 0))
    return pl.pallas_call(
        _merge_body,
        grid=(n // tm,),
        in_specs=[row(d), row(DA_V_W)] + [slabs] * 6 + [row(d), row(d),
                  _resident(w_proj_a.shape), _resident(w_proj_b.shape), _resident(w_out.shape),
                  _resident((1, d))],
        out_specs=row(d),
        out_shape=jax.ShapeDtypeStruct((n, d), _F32),
        compiler_params=_params(1),
        name="merge",
    )(x, oa, *dil_outs, *dil_lses, ga, gb, w_proj_a.astype(_BF16), w_proj_b.astype(_BF16),
      w_out.astype(_BF16), g_post.reshape(1, d))


def _rope_tables(positions):
    half = ROT_DIM // 2
    inv = ROPE_THETA ** (-(jnp.arange(0, ROT_DIM, 2, dtype=_F32) / ROT_DIM))
    ang = positions.astype(_F32).reshape(-1, 1) * inv
    cs = jnp.concatenate([jnp.cos(ang), jnp.sin(ang)], axis=1)
    terms = []
    for _ in range(3):
        head = lax.bitcast_convert_type(
            lax.bitcast_convert_type(cs, jnp.uint32) & jnp.uint32(0xFFFF0000), _F32)
        terms.append(head.astype(_BF16))
        cs = cs - head
    spread = np.zeros((ROT_DIM, 3 * LANES), np.float32)
    for lane in range(LANES):
        p = lane % DA_HEAD_DIM
        if p < ROT_DIM:
            spread[p % half, lane] = 1.0
        if half <= p < ROT_DIM:
            spread[p, LANES + lane] = 1.0
        if p < half:
            spread[half + p, 2 * LANES + lane] = -1.0
    return (jnp.concatenate(terms, axis=1),
            jnp.asarray(np.concatenate([spread] * 3, axis=0), dtype=_BF16))


def kernel(x, positions, w_in, lambda_q1, lambda_k1, lambda_q2, lambda_k2, g_subln, w_proj_a, w_proj_b, w_out, w_gu1, w_down1, w_gu2, w_down2, g_pre_ffn1, g_post_ffn1, g_pre_mix, g_post_mix, g_pre_ffn2, g_post_ffn2):
    b, s, d = x.shape
    n = b * s
    depth = w_in.shape[0]
    rope = _rope_tables(positions)
    xf = x.reshape(n, d)
    for l in range(depth):
        lambda_init = 0.8 - 0.6 * math.exp(-0.3 * l)
        xf = _ffn(xf, g_pre_ffn1[l], w_gu1[l], w_down1[l], g_post_ffn1[l])

        q, k, vt, qd, kd, vd, ga, gb = _inproj(xf, g_pre_mix[l], w_in[l], rope, b, s)
        lam_vecs = jnp.stack([lambda_q1[l], lambda_k1[l], lambda_q2[l], lambda_k2[l]], axis=0)
        oa = _diffattn(q.reshape(b, s, DA_QK_W), k.reshape(b, s, DA_QK_W), vt, lam_vecs,
                       g_subln[l], lambda_init)

        dil_outs, dil_lses = [], []
        for gi, (win, dil) in enumerate(DIL_PAIRS):
            o_g, lse_g = _dilattn(qd, kd, vd, gi, dil, win // (2 * dil), s)
            dil_outs.append(o_g)
            dil_lses.append(lse_g)

        xf = _merge(xf, oa.reshape(n, DA_V_W), dil_outs, dil_lses, ga, gb,
                    w_proj_a[l], w_proj_b[l], w_out[l], g_post_mix[l])
        xf = _ffn(xf, g_pre_ffn2[l], w_gu2[l], w_down2[l], g_post_ffn2[l])
    return xf.reshape(b, s, d)
```

```python
import functools
import math

import jax
import jax.numpy as jnp
import numpy as np
from jax import lax
from jax.experimental import pallas as pl
from jax.experimental.pallas import tpu as pltpu

D_MODEL = 1024
DA_HEADS = 8
DA_HEAD_DIM = 64
DA_V_DIM = 2 * DA_HEAD_DIM
DA_QK_W = DA_HEADS * 2 * DA_HEAD_DIM
DA_V_W = DA_HEADS * DA_V_DIM
DIL_PAIRS = ((128, 1), (512, 4), (2048, 16))
DIL_HEADS_PER_GROUP = 4
DIL_HEAD_DIM = 64
DIL_GROUP_W = DIL_HEADS_PER_GROUP * DIL_HEAD_DIM
DIL_W = DIL_GROUP_W * len(DIL_PAIRS)
ROPE_THETA = 500000.0
ROT_DIM = 16
D_FF = 2816
EPS = 1e-6
NEG = -1e30

LANES = 128
BF16_ROWS = 16
VMEM_LIMIT = 56 * 1024 * 1024

FFN_TM = 512
FFN_CHUNKS = 2
TOK_TM = 512
ATT_TQ = 512
ATT_TK = 2048
KEY_CHUNK = 256
DIL_T = 2048
DIL_MAX_STRIDE = 4

_BF16 = jnp.bfloat16
_F32 = jnp.float32


def _params(n_axes):
    return pltpu.CompilerParams(dimension_semantics=("arbitrary",) * n_axes,
                                vmem_limit_bytes=VMEM_LIMIT)


def _resident(shape):
    zeros = (0,) * len(shape)
    return pl.BlockSpec(shape, lambda *_: zeros, pipeline_mode=pl.Buffered(1))


def _rms(x, g):
    ms = jnp.mean(x * x, axis=-1, keepdims=True)
    return x * lax.rsqrt(ms + EPS) * g


def _ffn_body(x_ref, gpre_ref, wgu_ref, wd_ref, gpost_ref, o_ref):
    rows = x_ref.shape[0] // FFN_CHUNKS
    for i in range(FFN_CHUNKS):
        x = x_ref[i * rows:(i + 1) * rows, :]
        h = _rms(x, gpre_ref[...]).astype(_BF16)
        gu = jnp.dot(h, wgu_ref[...], preferred_element_type=_F32)
        g = gu[:, :D_FF]
        u = gu[:, D_FF:]
        a = (g * jax.nn.sigmoid(g) * u).astype(_BF16)
        y = jnp.dot(a, wd_ref[...], preferred_element_type=_F32)
        o_ref[i * rows:(i + 1) * rows, :] = x + 0.5 * _rms(y, gpost_ref[...])


def _ffn(x, g_pre, w_gu, w_down, g_post):
    n, d = x.shape
    tm = min(FFN_TM, n)
    row = pl.BlockSpec((tm, d), lambda i: (i, 0))
    return pl.pallas_call(
        _ffn_body,
        grid=(n // tm,),
        in_specs=[row, _resident((1, d)), _resident(w_gu.shape), _resident(w_down.shape),
                  _resident((1, d))],
        out_specs=row,
        out_shape=jax.ShapeDtypeStruct((n, d), _F32),
        compiler_params=_params(1),
        name="ffn",
    )(x, g_pre.reshape(1, d), w_gu.astype(_BF16), w_down.astype(_BF16), g_post.reshape(1, d))


def _inproj_body(x_ref, g_ref, cs_ref, spread_ref, w_ref, wvt_ref,
                 q_ref, k_ref, vt_ref, qd_ref, kd_ref, vd_ref, ga_ref, gb_ref, *, q_scale, qd_scale):
    h = _rms(x_ref[...], g_ref[...]).astype(_BF16)
    pats = jnp.dot(cs_ref[...], spread_ref[...], preferred_element_type=_F32)
    lane = lax.broadcasted_iota(jnp.int32, (1, LANES), 1)
    cos = pats[:, :LANES] + (lane % DA_HEAD_DIM >= ROT_DIM).astype(_F32)
    sa = pats[:, LANES:2 * LANES]
    sb = pats[:, 2 * LANES:]

    def proj(start, width):
        return jnp.dot(h, w_ref[:, start:start + width], preferred_element_type=_F32)

    def rope(zj):
        return (zj * cos + pltpu.roll(zj, ROT_DIM // 2, 1) * sa
                + pltpu.roll(zj, LANES - ROT_DIM // 2, 1) * sb)

    def rope_store(z, out_ref, scale):
        for j in range(z.shape[1] // LANES):
            r = rope(z[:, j * LANES:(j + 1) * LANES]) * scale
            out_ref[:, j * LANES:(j + 1) * LANES] = r.astype(out_ref.dtype)

    def slab_store(z, out_ref, fn):
        for j in range(z.shape[1] // LANES):
            out_ref[j] = fn(z[:, j * LANES:(j + 1) * LANES])

    c = 0
    rope_store(proj(c, DA_QK_W), q_ref, q_scale); c += DA_QK_W
    rope_store(proj(c, DA_QK_W), k_ref, 1.0); c += DA_QK_W
    vt_ref[...] = lax.dot_general(wvt_ref[...], h, (((1,), (1,)), ((), ())),
                                  preferred_element_type=_F32).astype(vt_ref.dtype)
    c += DA_V_W
    slab_store(proj(c, DIL_W), qd_ref, lambda zj: rope(zj) * qd_scale); c += DIL_W
    slab_store(proj(c, DIL_W), kd_ref, rope); c += DIL_W
    slab_store(proj(c, DIL_W), vd_ref, lambda zj: zj); c += DIL_W
    ga_ref[...] = proj(c, D_MODEL).astype(ga_ref.dtype); c += D_MODEL
    gb_ref[...] = proj(c, D_MODEL).astype(gb_ref.dtype)


def _transpose_cast_body(x_ref, o_ref):
    o_ref[...] = x_ref[...].T.astype(o_ref.dtype)


def _transpose_cast(x):
    rows, cols = x.shape
    tc = min(2 * LANES, cols)
    return pl.pallas_call(
        _transpose_cast_body,
        grid=(cols // tc,),
        in_specs=[pl.BlockSpec((rows, tc), lambda i: (0, i))],
        out_specs=pl.BlockSpec((tc, rows), lambda i: (i, 0)),
        out_shape=jax.ShapeDtypeStruct((cols, rows), _BF16),
        compiler_params=_params(1),
        name="wv_transpose",
    )(x)


def _inproj(x, g_pre, w_in, rope, batch, seq):
    n, d = x.shape
    tm = min(TOK_TM, seq)
    ns = seq // tm
    tk = min(ATT_TK, seq)
    per_key_tile = tk // tm
    cs, spread = rope
    w = w_in.astype(_BF16)
    v0 = 2 * DA_QK_W
    wvt = _transpose_cast(w_in[:, v0:v0 + DA_V_W])
    row = lambda width: pl.BlockSpec((tm, width), lambda i: (i, 0))
    nslab = DIL_W // LANES
    slabs = pl.BlockSpec((nslab, tm, LANES), lambda i: (0, i, 0))
    body = functools.partial(
        _inproj_body,
        q_scale=math.log2(math.e) / math.sqrt(DA_HEAD_DIM),
        qd_scale=1.0 / math.sqrt(DIL_HEAD_DIM))
    return pl.pallas_call(
        body,
        grid=(n // tm,),
        in_specs=[row(d), _resident((1, d)), row(cs.shape[1]), _resident(spread.shape),
                  _resident(w.shape), _resident(wvt.shape)],
        out_specs=[row(DA_QK_W), row(DA_QK_W),
                   pl.BlockSpec((None, None, DA_V_W, tm),
                                lambda i: (i // ns, (i % ns) // per_key_tile, 0, i % per_key_tile)),
                   slabs, slabs, slabs, row(d), row(d)],
        out_shape=[jax.ShapeDtypeStruct((n, DA_QK_W), _BF16),
                   jax.ShapeDtypeStruct((n, DA_QK_W), _BF16),
                   jax.ShapeDtypeStruct((batch, seq // tk, DA_V_W, tk), _BF16),
                   jax.ShapeDtypeStruct((nslab, n, LANES), _F32),
                   jax.ShapeDtypeStruct((nslab, n, LANES), _F32),
                   jax.ShapeDtypeStruct((nslab, n, LANES), _F32),
                   jax.ShapeDtypeStruct((n, d), _BF16),
                   jax.ShapeDtypeStruct((n, d), _BF16)],
        compiler_params=_params(1),
        name="inproj",
    )(x, g_pre.reshape(1, d), cs, spread, w, wvt)


def _diffattn_body(lam_ref, gsub_ref, q_ref, k_ref, vt_ref, o_ref,
                   s_ref, mx_ref, m_ref, acc_ref, fin_ref, *, tq, lambda_init):
    nk, _, tk = vt_ref.shape
    total = (q_ref.shape[0] // tq) * nk
    lane = lax.broadcasted_iota(jnp.int32, (tq, DA_V_DIM), 1)
    ones_rows = (lax.broadcasted_iota(jnp.int32, (BF16_ROWS, tk), 0) == 0).astype(_BF16)
    fin_ref[...] = jnp.ones(fin_ref.shape, _F32)

    def masked_q(t):
        q = q_ref[pl.ds(pl.multiple_of((t // nk) * tq, tq), tq), :]
        zero = jnp.zeros_like(q)
        return [jnp.where((lane >= c * DA_HEAD_DIM) & (lane < (c + 1) * DA_HEAD_DIM), q, zero)
                for c in range(2)]

    def score_chunk(qc, t, slot, c, kc):
        rows = pl.ds(pl.multiple_of((t % nk) * tk + kc * KEY_CHUNK, KEY_CHUNK), KEY_CHUNK)
        s = lax.dot_general(k_ref[rows, :], qc[c], (((1,), (1,)), ((), ())),
                            preferred_element_type=_F32)
        s_ref[slot, c, kc * KEY_CHUNK:(kc + 1) * KEY_CHUNK, :tq] = s
        return jnp.max(s, axis=0, keepdims=True)

    def scores(t, slot):
        qc = masked_q(t)
        for c in range(2):
            cm = [score_chunk(qc, t, slot, c, kc) for kc in range(tk // KEY_CHUNK)]
            mx_ref[slot, c] = functools.reduce(jnp.maximum, cm)

    def step(t, j, slot):
        t_next = jnp.minimum(t + 1, total - 1)
        qc = masked_q(t_next)
        vt = jnp.concatenate([vt_ref[j], ones_rows], axis=0)
        m_new, alpha = [], []
        for c in range(2):
            if j == 0:
                m_new.append(mx_ref[slot, c])
                alpha.append(None)
            else:
                m_old = m_ref[c]
                m_new.append(jnp.maximum(m_old, mx_ref[slot, c]))
                alpha.append(jnp.exp2(m_old - m_new[c]))
            m_ref[c] = m_new[c]
        cmax, pv = [None, None], [None, None]
        for kc in range(tk // KEY_CHUNK):
            keys = slice(kc * KEY_CHUNK, (kc + 1) * KEY_CHUNK)
            for c in range(2):
                p = jnp.exp2(s_ref[slot, c, keys, :tq] - m_new[c]).astype(_BF16)
                d = jnp.dot(vt[:, keys], p, preferred_element_type=_F32)
                pv[c] = d if pv[c] is None else pv[c] + d
                cm = score_chunk(qc, t_next, 1 - slot, c, kc)
                cmax[c] = cm if cmax[c] is None else jnp.maximum(cmax[c], cm)
        dst = fin_ref if j == nk - 1 else acc_ref
        for c in range(2):
            mx_ref[1 - slot, c] = cmax[c]
            dst[c] = pv[c] if j == 0 else alpha[c] * acc_ref[c] + pv[c]

    def finalize(qi):
        lq1, lk1, lq2, lk2 = (lam_ref[i:i + 1, :] for i in range(4))
        lam = (jnp.exp(jnp.sum(lq1 * lk1, axis=1, keepdims=True))
               - jnp.exp(jnp.sum(lq2 * lk2, axis=1, keepdims=True)) + lambda_init)
        num = [fin_ref[c, :DA_V_DIM, :] for c in range(2)]
        den = [fin_ref[c, DA_V_DIM:DA_V_DIM + 1, :] for c in range(2)]
        o = num[0] / den[0] - lam * (num[1] / den[1])
        ms = jnp.mean(o * o, axis=0, keepdims=True)
        y = o * lax.rsqrt(ms + EPS) * gsub_ref[...] * (1.0 - lambda_init)
        o_ref[pl.ds(pl.multiple_of(qi * tq, tq), tq), :] = y.T.astype(o_ref.dtype)

    scores(0, 0)

    def trip(qi, carry):
        finalize(jnp.maximum(qi - 1, 0))
        for j in range(nk):
            step(qi * nk + j, j, j % 2)
        return carry

    lax.fori_loop(0, total // nk, trip, 0)
    finalize(total // nk - 1)


def _diffattn(q, k, vt, lam_vecs, g_subln, lambda_init):
    b, s, _ = q.shape
    tq = min(ATT_TQ, s)
    nk, tk = vt.shape[1], vt.shape[3]
    assert nk % 2 == 0 and s % tq == 0 and tk % KEY_CHUNK == 0
    body = functools.partial(_diffattn_body, tq=tq, lambda_init=lambda_init)
    head = pl.BlockSpec((None, s, DA_V_DIM), lambda bi, h: (bi, 0, h))
    return pl.pallas_call(
        body,
        grid=(b, DA_HEADS),
        in_specs=[_resident(lam_vecs.shape), _resident((DA_V_DIM, 1)), head, head,
                  pl.BlockSpec((None, nk, DA_V_DIM, tk), lambda bi, h: (bi, 0, h, 0))],
        out_specs=head,
        out_shape=jax.ShapeDtypeStruct((b, s, DA_V_W), _BF16),
        scratch_shapes=[pltpu.VMEM((2, 2, tk, tq + LANES), _F32),
                        pltpu.VMEM((2, 2, 1, tq), _F32),
                        pltpu.VMEM((2, 1, tq), _F32),
                        pltpu.VMEM((2, DA_V_DIM + BF16_ROWS, tq), _F32),
                        pltpu.VMEM((2, DA_V_DIM + BF16_ROWS, tq), _F32)],
        compiler_params=_params(2),
        name="diffattn",
    )(lam_vecs, g_subln.reshape(DA_V_DIM, 1), q, k, vt)


def _dilattn_body(q_ref, kp_ref, km_ref, kn_ref, vp_ref, vm_ref, vn_ref, o_ref, lse_ref, *folded,
                  dil, half, seq_len, tiles_per_seq, fold):
    t = q_ref.shape[1]
    per_res = t // dil
    qb, win = 2 * half, 4 * half
    n_sub = per_res // qb
    nh = DIL_HEADS_PER_GROUP
    nslab = DIL_GROUP_W // LANES
    base = (pl.program_id(0) % tiles_per_seq) * per_res
    inner = dil // fold
    head_of_lane = lax.broadcasted_iota(jnp.int32, (qb, DIL_GROUP_W), 1) // DIL_HEAD_DIM
    r_iota = lax.broadcasted_iota(jnp.int32, (nh * qb, win), 0) % qb
    j = lax.broadcasted_iota(jnp.int32, (nh * qb, win), 1)
    band_bias = jnp.where((j >= r_iota) & (j <= r_iota + 2 * half), 0.0, NEG).astype(_F32)
    j_row = lax.broadcasted_iota(jnp.int32, (1, win), 1)

    def rows(start, count, stride=inner):
        return pl.ds(start, count, stride=stride) if stride > 1 else pl.ds(start, count)

    def gather(ref, start, count):
        return jnp.concatenate([ref[sl, rows(start, count), :] for sl in range(nslab)],
                               axis=1).astype(_BF16)

    def window(prev_ref, main_ref, next_ref, res, i, row0):
        lo = (gather(prev_ref, res, half) if i == 0
              else gather(main_ref, row0 - half * dil, half))
        hi = (gather(next_ref, res, half) if i == n_sub - 1
              else gather(main_ref, row0 + qb * dil, half))
        return jnp.concatenate([lo, gather(main_ref, row0, qb), hi], axis=0)

    if fold > 1:
        qf_ref, kf_ref, vf_ref, of_ref, lf_ref = folded
        tf, hf = t // fold, half * inner
        for a in range(fold):
            for sl in range(nslab):
                qf_ref[a, sl] = q_ref[sl, rows(a, tf, fold), :]
                for dst, (p_ref, m_ref, n_ref) in ((kf_ref, (kp_ref, km_ref, kn_ref)),
                                                   (vf_ref, (vp_ref, vm_ref, vn_ref))):
                    dst[a, sl, 0:hf] = p_ref[sl, rows(a, hf, fold), :]
                    dst[a, sl, hf:hf + tf] = m_ref[sl, rows(a, tf, fold), :]
                    dst[a, sl, hf + tf:hf + tf + hf] = n_ref[sl, rows(a, hf, fold), :]

    def block(idx, carry):
        res, i = idx // n_sub, idx % n_sub
        if fold > 1:
            a = res % fold
            row0 = i * (qb * inner) + res // fold
            qi = gather(qf_ref.at[a], row0, qb)
            kw = gather(kf_ref.at[a], row0, win)
            vw = gather(vf_ref.at[a], row0, win)
            o_dst, lse_dst = of_ref.at[a], lf_ref.at[a]
        else:
            row0 = i * (qb * dil) + res
            qi = gather(q_ref, row0, qb)
            kw = window(kp_ref, km_ref, kn_ref, res, i, row0)
            vw = window(vp_ref, vm_ref, vn_ref, res, i, row0)
            o_dst, lse_dst = o_ref, lse_ref
        s_bias = band_bias
        if i == 0 or i == n_sub - 1:
            kpos = base + (i * qb - half) + j_row
            in_seq = (kpos >= 0) & (kpos < seq_len)
            s_bias = jnp.where(in_seq, band_bias, NEG)
        zero = jnp.zeros_like(qi)
        qs = jnp.concatenate([jnp.where(head_of_lane == h, qi, zero) for h in range(nh)], axis=0)
        s = lax.dot_general(qs, kw, (((1,), (1,)), ((), ())), preferred_element_type=_F32)
        s = s + s_bias
        m = jnp.max(s, axis=1, keepdims=True)
        e = jnp.exp(s - m)
        l = jnp.sum(e, axis=1, keepdims=True)
        pv = jnp.dot(e.astype(_BF16), vw, preferred_element_type=_F32) / l
        lse = m + jnp.log(l)
        o = jnp.zeros((qb, DIL_GROUP_W), _F32)
        lse_full = jnp.zeros((qb, DIL_GROUP_W), _F32)
        for h in range(nh):
            sel = head_of_lane == h
            o = jnp.where(sel, pv[h * qb:(h + 1) * qb], o)
            lse_full = jnp.where(sel, lse[h * qb:(h + 1) * qb], lse_full)
        for sl in range(nslab):
            o_dst[sl, rows(row0, qb), :] = o[:, sl * LANES:(sl + 1) * LANES]
            lse_dst[sl, rows(row0, qb), :] = lse_full[:, sl * LANES:(sl + 1) * LANES]
        return carry

    for idx in range(dil * n_sub):
        block(idx, None)

    if fold > 1:
        for a in range(fold):
            for sl in range(nslab):
                o_ref[sl, rows(a, tf, fold), :] = of_ref[a, sl]
                lse_ref[sl, rows(a, tf, fold), :] = lf_ref[a, sl]


def _dilattn(qd, kd, vd, group, dil, half, seq):
    nslab_all, n, _ = qd.shape
    nslab = DIL_GROUP_W // LANES
    t = min(DIL_T, seq)
    halo = half * dil
    assert seq % t == 0 and t % (2 * half * dil) == 0 and t % halo == 0 and half % 8 == 0
    tiles_per_seq = seq // t
    per_tile = t // halo
    last = n // halo - 1
    main = pl.BlockSpec((nslab, t, LANES), lambda i: (group, i, 0))
    prev = pl.BlockSpec((nslab, halo, LANES),
                        lambda i: (group, jnp.maximum(i * per_tile - 1, 0), 0))
    nxt = pl.BlockSpec((nslab, halo, LANES),
                       lambda i: (group, jnp.minimum((i + 1) * per_tile, last), 0))
    out = pl.BlockSpec((nslab, t, LANES), lambda i: (0, i, 0))
    fold = max(dil // DIL_MAX_STRIDE, 1)
    assert dil % fold == 0 and halo % fold == 0
    body = functools.partial(_dilattn_body, dil=dil, half=half, seq_len=seq // dil,
                             tiles_per_seq=tiles_per_seq, fold=fold)
    folded = []
    if fold > 1:
        tile = pltpu.VMEM((fold, nslab, t // fold, LANES), _F32)
        with_halos = pltpu.VMEM((fold, nslab, (t + 2 * halo) // fold, LANES), _F32)
        folded = [tile, with_halos, with_halos, tile, tile]
    return pl.pallas_call(
        body,
        grid=(n // t,),
        in_specs=[main, prev, main, nxt, prev, main, nxt],
        out_specs=[out, out],
        out_shape=[jax.ShapeDtypeStruct((nslab, n, LANES), _F32)] * 2,
        scratch_shapes=folded,
        compiler_params=_params(1),
        name=f"dilattn_d{dil}",
    )(qd, kd, kd, kd, vd, vd, vd)


def _merge_body(x_ref, oa_ref, o0_ref, o1_ref, o2_ref, l0_ref, l1_ref, l2_ref, ga_ref, gb_ref,
                wa_ref, wb_ref, wo_ref, g_ref, out_ref):
    od_slabs = []
    for sl in range(DIL_GROUP_W // LANES):
        lses = (l0_ref[sl], l1_ref[sl], l2_ref[sl])
        outs = (o0_ref[sl], o1_ref[sl], o2_ref[sl])
        m = jnp.maximum(jnp.maximum(lses[0], lses[1]), lses[2])
        es = [jnp.exp(l - m) for l in lses]
        den = es[0] + es[1] + es[2]
        od_slabs.append((es[0] / den) * outs[0] + (es[1] / den) * outs[1]
                        + (es[2] / den) * outs[2])
    od = jnp.concatenate(od_slabs, axis=1)
    odb = od.astype(_BF16)
    halves = []
    width = wa_ref.shape[1] // 2
    for hcol in range(2):
        cols = slice(hcol * width, (hcol + 1) * width)
        pa = jnp.dot(oa_ref[...], wa_ref[:, cols], preferred_element_type=_F32)
        pb = jnp.dot(odb, wb_ref[:, cols], preferred_element_type=_F32)
        halves.append((jax.nn.sigmoid(ga_ref[:, cols].astype(_F32)) * pa
                       + jax.nn.sigmoid(gb_ref[:, cols].astype(_F32)) * pb).astype(_BF16))
    mo = jnp.dot(jnp.concatenate(halves, axis=1), wo_ref[...], preferred_element_type=_F32)
    out_ref[...] = x_ref[...] + _rms(mo, g_ref[...])


def _merge(x, oa, dil_outs, dil_lses, ga, gb, w_proj_a, w_proj_b, w_out, g_post):
    n, d = x.shape
    tm = min(TOK_TM, n)
    row = lambda width: pl.BlockSpec((tm, width), lambda i: (i, 0))
    slabs = pl.BlockSpec((DIL_GROUP_W // LANES, tm, LANES), lambda i: (0, i, 0))
    return pl.pallas_call(
        _merge_body,
        grid=(n // tm,),
        in_specs=[row(d), row(DA_V_W)] + [slabs] * 6 + [row(d), row(d),
                  _resident(w_proj_a.shape), _resident(w_proj_b.shape), _resident(w_out.shape),
                  _resident((1, d))],
        out_specs=row(d),
        out_shape=jax.ShapeDtypeStruct((n, d), _F32),
        compiler_params=_params(1),
        name="merge",
    )(x, oa, *dil_outs, *dil_lses, ga, gb, w_proj_a.astype(_BF16), w_proj_b.astype(_BF16),
      w_out.astype(_BF16), g_post.reshape(1, d))


def _rope_tables(positions):
    half = ROT_DIM // 2
    inv = ROPE_THETA ** (-(jnp.arange(0, ROT_DIM, 2, dtype=_F32) / ROT_DIM))
    ang = positions.astype(_F32).reshape(-1, 1) * inv
    cs = jnp.concatenate([jnp.cos(ang), jnp.sin(ang)], axis=1)
    terms = []
    for _ in range(3):
        head = lax.bitcast_convert_type(
            lax.bitcast_convert_type(cs, jnp.uint32) & jnp.uint32(0xFFFF0000), _F32)
        terms.append(head.astype(_BF16))
        cs = cs - head
    spread = np.zeros((ROT_DIM, 3 * LANES), np.float32)
    for lane in range(LANES):
        p = lane % DA_HEAD_DIM
        if p < ROT_DIM:
            spread[p % half, lane] = 1.0
        if half <= p < ROT_DIM:
            spread[p, LANES + lane] = 1.0
        if p < half:
            spread[half + p, 2 * LANES + lane] = -1.0
    return (jnp.concatenate(terms, axis=1),
            jnp.asarray(np.concatenate([spread] * 3, axis=0), dtype=_BF16))


def kernel(x, positions, w_in, lambda_q1, lambda_k1, lambda_q2, lambda_k2, g_subln, w_proj_a, w_proj_b, w_out, w_gu1, w_down1, w_gu2, w_down2, g_pre_ffn1, g_post_ffn1, g_pre_mix, g_post_mix, g_pre_ffn2, g_post_ffn2):
    b, s, d = x.shape
    n = b * s
    depth = w_in.shape[0]
    rope = _rope_tables(positions)
    xf = x.reshape(n, d)
    for l in range(depth):
        lambda_init = 0.8 - 0.6 * math.exp(-0.3 * l)
        xf = _ffn(xf, g_pre_ffn1[l], w_gu1[l], w_down1[l], g_post_ffn1[l])

        q, k, vt, qd, kd, vd, ga, gb = _inproj(xf, g_pre_mix[l], w_in[l], rope, b, s)
        lam_vecs = jnp.stack([lambda_q1[l], lambda_k1[l], lambda_q2[l], lambda_k2[l]], axis=0)
        oa = _diffattn(q.reshape(b, s, DA_QK_W), k.reshape(b, s, DA_QK_W), vt, lam_vecs,
                       g_subln[l], lambda_init)

        dil_outs, dil_lses = [], []
        for gi, (win, dil) in enumerate(DIL_PAIRS):
            o_g, lse_g = _dilattn(qd, kd, vd, gi, dil, win // (2 * dil), s)
            dil_outs.append(o_g)
            dil_lses.append(lse_g)

        xf = _merge(xf, oa.reshape(n, DA_V_W), dil_outs, dil_lses, ga, gb,
                    w_proj_a[l], w_proj_b[l], w_out[l], g_post_mix[l])
        xf = _ffn(xf, g_pre_ffn2[l], w_gu2[l], w_down2[l], g_post_ffn2[l])
    return xf.reshape(b, s, d)
```
